```python
import jax, jax.numpy as jnp
from jax import lax
import numpy as np

D_MODEL = 1024
BATCH = 32
SEQ = 2048
DEPTH = 1

N_HEADS = 16
HEAD_DIM = D_MODEL // N_HEADS
D_ATTN = N_HEADS * HEAD_DIM
BLOCK_Q = 128
SB_SCALE = HEAD_DIM ** -0.5
POOL_WINDOWS = (2, 4, 8, 16)
N_POOL_GROUPS = len(POOL_WINDOWS)
D_POOL = D_MODEL
POOL_GROUP_DIM = D_POOL // N_POOL_GROUPS
SPLITS = (D_ATTN, D_ATTN, D_ATTN, D_ATTN, D_POOL, D_POOL, D_MODEL, D_MODEL)
IN_COLS = sum(SPLITS)
EPS = 1e-6

kernel_name = "hybrid_stickbreak_pool_block"


def rms_norm(x, gain):
    xf = x.astype(jnp.float32)
    y = xf * lax.rsqrt(jnp.mean(xf * xf, axis=-1, keepdims=True) + EPS)
    return (y * gain.astype(jnp.float32)).astype(x.dtype)


def stick_breaking_attention(q, k, v):
    S = q.shape[2]
    outs = []
    for i in range(S // BLOCK_Q):
        q0, q1 = i * BLOCK_Q, (i + 1) * BLOCK_Q
        qb = q[:, :, q0:q1]
        kb = k[:, :, :q1]
        vb = v[:, :, :q1]
        z = jnp.einsum('bhqd,bhkd->bhqk', qb, kb).astype(jnp.float32) * SB_SCALE
        q_pos = jnp.arange(q0, q1)[:, None]
        k_pos = jnp.arange(q1)[None, :]
        mask = k_pos < q_pos
        log_1m_beta = jnp.where(mask, -jax.nn.softplus(z), 0.0)
        suffix = lax.cumsum(log_1m_beta, axis=3, reverse=True) - log_1m_beta
        w = jnp.where(mask, jnp.exp(jax.nn.log_sigmoid(z) + suffix), 0.0)
        outs.append(jnp.einsum('bhqk,bhkd->bhqd', w.astype(vb.dtype), vb))
    return jnp.concatenate(outs, axis=2)


def multiscale_causal_pool(u, pool_w, pool_scale):
    B, S, _ = u.shape
    uf = u.astype(jnp.float32)
    pos_count = jnp.arange(S) + 1
    diffs = []
    for g, win in enumerate(POOL_WINDOWS):
        ug = uf[:, :, g * POOL_GROUP_DIM:(g + 1) * POOL_GROUP_DIM]
        cs = jnp.cumsum(ug, axis=1)
        cs_shift = jnp.pad(cs, ((0, 0), (win, 0), (0, 0)))[:, :S]
        count = jnp.minimum(pos_count, win).astype(jnp.float32)[None, :, None]
        diffs.append((cs - cs_shift) / count - ug)
    d = jnp.stack(diffs, axis=2).astype(u.dtype)
    mixed = jnp.einsum('bsgc,gcd->bsgd', d, pool_w).reshape(B, S, D_POOL)
    return mixed * pool_scale


def hybrid_layer(x, c, norm_gain, w_ada, b_ada, w_in, pool_w, pool_scale,
                 w_branch_a, w_branch_b, w_out):
    B, S, D = x.shape
    mod = jax.nn.silu(c) @ w_ada + b_ada
    shift, scale, gate = jnp.split(mod, 3, axis=-1)
    h = rms_norm(x, norm_gain) * (1.0 + scale[:, None, :]) + shift[:, None, :]

    proj = h @ w_in
    offs = np.cumsum(SPLITS)[:-1].tolist()
    q, k, v, z_a, u, z_b, m_a, m_b = jnp.split(proj, offs, axis=-1)

    to_heads = lambda t: t.reshape(B, S, N_HEADS, HEAD_DIM).transpose(0, 2, 1, 3)
    attn = stick_breaking_attention(to_heads(q), to_heads(k), to_heads(v))
    attn = attn.transpose(0, 2, 1, 3).reshape(B, S, D_ATTN)
    p_a = (attn * jax.nn.silu(z_a)) @ w_branch_a

    pooled = multiscale_causal_pool(u, pool_w, pool_scale)
    p_b = (pooled * jax.nn.silu(z_b)) @ w_branch_b

    merged = jax.nn.sigmoid(m_a) * p_a + jax.nn.sigmoid(m_b) * p_b
    out = merged @ w_out
    return x + gate[:, None, :] * out


def setup_inputs(seed: int = 0) -> dict:
    key = jax.random.key(seed)
    ks = jax.random.split(key, 13)
    f32 = jnp.float32
    nrm = lambda k, shape, s: jax.random.normal(k, shape, f32) * s
    return {
        "x": nrm(ks[0], (BATCH, SEQ, D_MODEL), 1.0),
        "c": nrm(ks[1], (BATCH, D_MODEL), 1.0),
        "norm_gain": 1.0 + nrm(ks[2], (DEPTH, D_MODEL), 0.05),
        "w_ada": nrm(ks[3], (DEPTH, D_MODEL, 3 * D_MODEL), 0.5 * D_MODEL ** -0.5),
        "b_ada": nrm(ks[4], (DEPTH, 3 * D_MODEL), 0.01),
        "w_in": nrm(ks[5], (DEPTH, D_MODEL, IN_COLS), D_MODEL ** -0.5),
        "pool_w": nrm(ks[6], (DEPTH, N_POOL_GROUPS, POOL_GROUP_DIM, POOL_GROUP_DIM), POOL_GROUP_DIM ** -0.5),
        "pool_scale": 1.0 + nrm(ks[7], (DEPTH, D_POOL), 0.1),
        "w_branch_a": nrm(ks[8], (DEPTH, D_ATTN, D_MODEL), D_ATTN ** -0.5),
        "w_branch_b": nrm(ks[9], (DEPTH, D_POOL, D_MODEL), D_POOL ** -0.5),
        "w_out": nrm(ks[10], (DEPTH, D_MODEL, D_MODEL), D_MODEL ** -0.5),
        "final_gain": 1.0 + nrm(ks[11], (D_MODEL,), 0.05),
    }


def reference(x, c, norm_gain, w_ada, b_ada, w_in, pool_w, pool_scale,
              w_branch_a, w_branch_b, w_out, final_gain):
    for l in range(DEPTH):
        x = hybrid_layer(x, c, norm_gain[l], w_ada[l], b_ada[l], w_in[l], pool_w[l],
                         pool_scale[l], w_branch_a[l], w_branch_b[l], w_out[l])
    return rms_norm(x, final_gain)
```

```python
import functools

import jax
import jax.numpy as jnp
from jax import lax
from jax.experimental import pallas as pl
from jax.experimental.pallas import tpu as pltpu

N_HEADS = 16
HEAD_DIM = 64
SB_SCALE = HEAD_DIM ** -0.5
POOL_WINDOWS = (2, 4, 8, 16)
MAX_WINDOW = max(POOL_WINDOWS)
EPS = 1e-6

LANES = 128
PROJ_COL_TILE = 1024
PROJ_ROW_CHUNK = 512
NORM_ROW_CHUNK = 64
ATTN_BLOCK = 256
TAIL_ROWS = 256
VMEM_LIMIT = 48 * 1024 * 1024

F32 = jnp.float32
BF16 = jnp.bfloat16


def _silu(z):
    return z * jax.nn.sigmoid(z)


def _mod_kernel(c_ref, w_ref, b_ref, o_ref):
    c = c_ref[...]
    o_ref[...] = jnp.dot(_silu(c), w_ref[...], preferred_element_type=F32,
                         precision=lax.Precision.HIGHEST) + b_ref[...]


def _modulation(c, w_ada, b_ada):
    batch, d = c.shape
    n = w_ada.shape[1]
    return pl.pallas_call(
        _mod_kernel,
        out_shape=jax.ShapeDtypeStruct((batch, n), F32),
        grid=(n // d,),
        in_specs=[pl.BlockSpec((batch, d), lambda j: (0, 0)),
                  pl.BlockSpec((d, d), lambda j: (0, j)),
                  pl.BlockSpec((1, d), lambda j: (0, j))],
        out_specs=pl.BlockSpec((batch, d), lambda j: (0, j)),
        compiler_params=pltpu.CompilerParams(dimension_semantics=("arbitrary",),
                                             vmem_limit_bytes=VMEM_LIMIT),
        name="adaln_mod",
    )(c, w_ada, b_ada.reshape(1, n))


def _proj_kernel(x_ref, gain_ref, scale_ref, shift_ref, w_ref, o_ref, h_ref):
    seq = x_ref.shape[0]

    @pl.when(pl.program_id(1) == 0)
    def _():
        gain = gain_ref[...]
        mul = 1.0 + scale_ref[...]
        shift = shift_ref[...]

        def body(r, carry):
            rows = pl.ds(pl.multiple_of(r * NORM_ROW_CHUNK, NORM_ROW_CHUNK), NORM_ROW_CHUNK)
            xf = x_ref[rows, :]
            y = xf * lax.rsqrt(jnp.mean(xf * xf, axis=-1, keepdims=True) + EPS)
            h_ref[rows, :] = ((y * gain) * mul + shift).astype(BF16)
            return carry

        lax.fori_loop(0, seq // NORM_ROW_CHUNK, body, 0)

    for m in range(seq // PROJ_ROW_CHUNK):
        rows = slice(m * PROJ_ROW_CHUNK, (m + 1) * PROJ_ROW_CHUNK)
        o_ref[rows, :] = jnp.dot(h_ref[rows, :], w_ref[...],
                                 preferred_element_type=F32).astype(BF16)


def _projection(x, norm_gain, scale, shift, w_in_bf16):
    batch, seq, d = x.shape
    n = w_in_bf16.shape[1]
    return pl.pallas_call(
        _proj_kernel,
        out_shape=jax.ShapeDtypeStruct((batch, seq, n), BF16),
        grid=(batch, n // PROJ_COL_TILE),
        in_specs=[pl.BlockSpec((None, seq, d), lambda b, j: (b, 0, 0)),
                  pl.BlockSpec((1, d), lambda b, j: (0, 0)),
                  pl.BlockSpec((None, 1, d), lambda b, j: (b, 0, 0)),
                  pl.BlockSpec((None, 1, d), lambda b, j: (b, 0, 0)),
                  pl.BlockSpec((d, PROJ_COL_TILE), lambda b, j: (0, j))],
        out_specs=pl.BlockSpec((None, seq, PROJ_COL_TILE), lambda b, j: (b, 0, j)),
        scratch_shapes=[pltpu.VMEM((seq, d), BF16)],
        compiler_params=pltpu.CompilerParams(dimension_semantics=("arbitrary", "arbitrary"),
                                             vmem_limit_bytes=VMEM_LIMIT),
        name="in_proj",
    )(x, norm_gain.reshape(1, d), scale.reshape(batch, 1, d), shift.reshape(batch, 1, d),
      w_in_bf16)


def _attn_kernel(q_ref, k_ref, v_ref, tri_ref, o_ref, acc_ref):
    blk = q_ref.shape[0]
    i = pl.program_id(2)
    first_head = lax.broadcasted_iota(jnp.int32, (1, LANES), 1) < HEAD_DIM
    q = q_ref[...] * SB_SCALE
    zero = jnp.zeros_like(q)
    q_heads = (jnp.where(first_head, q, zero), jnp.where(first_head, zero, q))
    tri = tri_ref[...]
    row = lax.broadcasted_iota(jnp.int32, (blk, blk), 0)
    col = lax.broadcasted_iota(jnp.int32, (blk, blk), 1)
    causal = col < row

    def block(j, carries, masked):
        keys = pl.ds(pl.multiple_of(j * blk, blk), blk)
        kb = k_ref[keys, :]
        vb = v_ref[keys, :]
        vzero = jnp.zeros_like(vb)
        v_heads = (jnp.where(first_head, vb, vzero), jnp.where(first_head, vzero, vb))
        out = []
        pv = None
        for h in range(2):
            z = lax.dot_general(q_heads[h], kb, (((1,), (1,)), ((), ())),
                                preferred_element_type=F32)
            sp = jnp.maximum(z, 0.0) + jnp.log(1.0 + jnp.exp(-jnp.abs(z)))
            log_sig = z - sp
            if masked:
                sp = jnp.where(causal, sp, 0.0)
            hi = sp.astype(BF16)
            lo = (sp - hi.astype(F32)).astype(BF16)
            later = (jnp.dot(hi, tri, preferred_element_type=F32)
                     + jnp.dot(lo, tri, preferred_element_type=F32))
            w = jnp.exp(log_sig - later - carries[h])
            if masked:
                w = jnp.where(causal, w, 0.0)
            out.append(carries[h] + later[:, :1] + sp[:, :1])
            d = jnp.dot(w.astype(BF16), v_heads[h], preferred_element_type=F32)
            pv = d if pv is None else pv + d
        acc_ref[...] += pv
        return tuple(out)

    acc_ref[...] = jnp.zeros_like(acc_ref)
    carry0 = (jnp.zeros((blk, 1), F32), jnp.zeros((blk, 1), F32))
    carries = block(i, carry0, True)

    def body(n, carries):
        return block(i - 1 - n, carries, False)

    lax.fori_loop(0, i, body, carries)
    o_ref[...] = acc_ref[...].astype(o_ref.dtype)


def _attention(proj, d_model):
    batch, seq, _ = proj.shape
    blk = ATTN_BLOCK
    pairs = d_model // LANES
    tri = (jnp.arange(blk)[:, None] > jnp.arange(blk)[None, :]).astype(BF16)
    return pl.pallas_call(
        _attn_kernel,
        out_shape=jax.ShapeDtypeStruct((batch, seq, d_model), BF16),
        grid=(batch, pairs, seq // blk),
        in_specs=[pl.BlockSpec((None, blk, LANES), lambda b, p, i: (b, i, p)),
                  pl.BlockSpec((None, seq, LANES), lambda b, p, i: (b, 0, pairs + p)),
                  pl.BlockSpec((None, seq, LANES), lambda b, p, i: (b, 0, 2 * pairs + p)),
                  pl.BlockSpec((blk, blk), lambda b, p, i: (0, 0))],
        out_specs=pl.BlockSpec((None, blk, LANES), lambda b, p, i: (b, i, p)),
        scratch_shapes=[pltpu.VMEM((blk, LANES), F32)],
        compiler_params=pltpu.CompilerParams(
            dimension_semantics=("arbitrary", "arbitrary", "arbitrary"),
            vmem_limit_bytes=VMEM_LIMIT),
        name="sb_attention",
    )(proj, proj, proj, tri)


def _tail_kernel(attn_ref, za_ref, u_ref, uprev_ref, zb_ref, ma_ref, mb_ref, x_ref, gate_ref,
                 band_ref, poolw_ref, pscale_ref, wa_ref, wb_ref, wo_ref, fgain_ref,
                 o_ref, uext_ref, pooled_ref):
    rows = attn_ref.shape[0]
    d = attn_ref.shape[1]
    group = d // len(POOL_WINDOWS)
    i = pl.program_id(1)

    a_in = attn_ref[...].astype(F32) * _silu(za_ref[...].astype(F32))
    p_a = jnp.dot(a_in.astype(BF16), wa_ref[...], preferred_element_type=F32)

    prev = uprev_ref[...]
    uext_ref[0:MAX_WINDOW, :] = jnp.where(i > 0, prev, jnp.zeros_like(prev))
    uext_ref[MAX_WINDOW:, :] = u_ref[...]
    pos = lax.broadcasted_iota(jnp.int32, (rows, 1), 0) + i * rows + 1
    for g, win in enumerate(POOL_WINDOWS):
        cols = slice(g * group, (g + 1) * group)
        wsum = jnp.dot(band_ref[g], uext_ref[:, cols], preferred_element_type=F32)
        inv_count = 1.0 / jnp.minimum(pos, win).astype(F32)
        diff = wsum * inv_count - u_ref[:, cols].astype(F32)
        mixed = jnp.dot(diff.astype(BF16), poolw_ref[g], preferred_element_type=F32)
        pooled_ref[:, cols] = mixed * pscale_ref[:, cols]
    b_in = pooled_ref[...] * _silu(zb_ref[...].astype(F32))
    p_b = jnp.dot(b_in.astype(BF16), wb_ref[...], preferred_element_type=F32)

    merged = (jax.nn.sigmoid(ma_ref[...].astype(F32)) * p_a
              + jax.nn.sigmoid(mb_ref[...].astype(F32)) * p_b)
    out = jnp.dot(merged.astype(BF16), wo_ref[...], preferred_element_type=F32)
    y = x_ref[...] + gate_ref[...] * out
    y = y * lax.rsqrt(jnp.mean(y * y, axis=-1, keepdims=True) + EPS)
    o_ref[...] = y * fgain_ref[...]


def _tail(attn, proj, x, gate, pool_w, pool_scale, w_branch_a, w_branch_b, w_out, final_gain):
    batch, seq, d = x.shape
    rows = TAIL_ROWS
    n_win = len(POOL_WINDOWS)
    group = d // n_win
    t = jnp.arange(rows)[:, None] + MAX_WINDOW
    tp = jnp.arange(rows + MAX_WINDOW)[None, :]
    band = jnp.stack([((tp <= t) & (tp > t - w)) for w in POOL_WINDOWS]).astype(BF16)

    tile = lambda col: pl.BlockSpec((None, rows, d), lambda b, i: (b, i, col))
    prev_blocks = rows // MAX_WINDOW
    full2 = lambda shape: pl.BlockSpec(shape, lambda b, i: (0, 0))
    full3 = lambda shape: pl.BlockSpec(shape, lambda b, i: (0, 0, 0))
    return pl.pallas_call(
        _tail_kernel,
        out_shape=jax.ShapeDtypeStruct((batch, seq, d), F32),
        grid=(batch, seq // rows),
        in_specs=[tile(0),
                  tile(3),
                  tile(4),
                  pl.BlockSpec((None, MAX_WINDOW, d),
                               lambda b, i: (b, jnp.maximum(i * prev_blocks - 1, 0), 4)),
                  tile(5), tile(6), tile(7),
                  tile(0),
                  pl.BlockSpec((None, 1, d), lambda b, i: (b, 0, 0)),
                  full3((n_win, rows, rows + MAX_WINDOW)),
                  full3((n_win, group, group)),
                  full2((1, d)),
                  full2((d, d)), full2((d, d)), full2((d, d)),
                  full2((1, d))],
        out_specs=pl.BlockSpec((None, rows, d), lambda b, i: (b, i, 0)),
        scratch_shapes=[pltpu.VMEM((rows + MAX_WINDOW, d), BF16),
                        pltpu.VMEM((rows, d), F32)],
        compiler_params=pltpu.CompilerParams(dimension_semantics=("arbitrary", "arbitrary"),
                                             vmem_limit_bytes=VMEM_LIMIT),
        name="tail",
    )(attn, proj, proj, proj, proj, proj, proj, x, gate.reshape(batch, 1, d), band,
      pool_w.astype(BF16), pool_scale.reshape(1, d), w_branch_a.astype(BF16),
      w_branch_b.astype(BF16), w_out.astype(BF16), final_gain.reshape(1, d))


def kernel(x, c, norm_gain, w_ada, b_ada, w_in, pool_w, pool_scale, w_branch_a, w_branch_b,
           w_out, final_gain):
    depth = norm_gain.shape[0]
    assert depth == 1, "single-layer block"
    d = x.shape[-1]
    mod = _modulation(c, w_ada[0], b_ada[0])
    shift, scale, gate = mod[:, :d], mod[:, d:2 * d], mod[:, 2 * d:]
    proj = _projection(x, norm_gain[0], scale, shift, w_in[0].astype(BF16))
    attn = _attention(proj, d)
    return _tail(attn, proj, x, gate, pool_w[0], pool_scale[0], w_branch_a[0], w_branch_b[0],
                 w_out[0], final_gain)
```

```python
import jax
import jax.numpy as jnp
from jax import lax
from jax.experimental import pallas as pl
from jax.experimental.pallas import tpu as pltpu

N_HEADS = 16
HEAD_DIM = 64
SB_SCALE = HEAD_DIM ** -0.5
POOL_WINDOWS = (2, 4, 8, 16)
MAX_WINDOW = max(POOL_WINDOWS)
EPS = 1e-6

LANES = 128
PROJ_COL_TILE = 1024
PROJ_ROW_CHUNK = 512
NORM_ROW_CHUNK = 64
ATTN_ROWS = 256
ATTN_KEYS = 128
ATTN_PAD = 256
TAIL_ROWS = 256
VMEM_LIMIT = 48 * 1024 * 1024
DEAD_MASS = 104.0
EXP_CLAMP = 88.0

F32 = jnp.float32
BF16 = jnp.bfloat16


def _silu(z):
    return z * jax.nn.sigmoid(z)


def _mod_kernel(c_ref, w_ref, b_ref, o_ref):
    c = c_ref[...]
    o_ref[...] = jnp.dot(_silu(c), w_ref[...], preferred_element_type=F32,
                         precision=lax.Precision.HIGHEST) + b_ref[...]


def _modulation(c, w_ada, b_ada):
    batch, d = c.shape
    n = w_ada.shape[1]
    return pl.pallas_call(
        _mod_kernel,
        out_shape=jax.ShapeDtypeStruct((batch, n), F32),
        grid=(n // d,),
        in_specs=[pl.BlockSpec((batch, d), lambda j: (0, 0)),
                  pl.BlockSpec((d, d), lambda j: (0, j)),
                  pl.BlockSpec((1, d), lambda j: (0, j))],
        out_specs=pl.BlockSpec((batch, d), lambda j: (0, j)),
        compiler_params=pltpu.CompilerParams(dimension_semantics=("arbitrary",),
                                             vmem_limit_bytes=VMEM_LIMIT),
        name="adaln_mod",
    )(c, w_ada, b_ada.reshape(1, n))


N_QKV_TILES = 3


def _proj_kernel(x_ref, gain_ref, scale_ref, shift_ref, w_ref, qkv_ref, rest_ref, h_ref):
    seq = x_ref.shape[0]
    j = pl.program_id(1)
    pairs = qkv_ref.shape[0]

    @pl.when(j == 0)
    def _():
        gain = gain_ref[...]
        mul = 1.0 + scale_ref[...]
        shift = shift_ref[...]

        def body(r, carry):
            rows = pl.ds(pl.multiple_of(r * NORM_ROW_CHUNK, NORM_ROW_CHUNK), NORM_ROW_CHUNK)
            xf = x_ref[rows, :]
            y = xf * lax.rsqrt(jnp.mean(xf * xf, axis=-1, keepdims=True) + EPS)
            h_ref[rows, :] = ((y * gain) * mul + shift).astype(BF16)
            return carry

        lax.fori_loop(0, seq // NORM_ROW_CHUNK, body, 0)

    def chunk(m):
        rows = slice(m * PROJ_ROW_CHUNK, (m + 1) * PROJ_ROW_CHUNK)
        return rows, jnp.dot(h_ref[rows, :], w_ref[...], preferred_element_type=F32)

    @pl.when(j < N_QKV_TILES)
    def _():
        qkv_ref[:, 0:ATTN_PAD, :] = jnp.zeros((pairs, ATTN_PAD, LANES), BF16)
        for m in range(seq // PROJ_ROW_CHUNK):
            rows, res = chunk(m)
            res = res.astype(BF16)
            for p in range(pairs):
                qkv_ref[p, ATTN_PAD + rows.start:ATTN_PAD + rows.stop, :] = (
                    res[:, p * LANES:(p + 1) * LANES])

    @pl.when(j >= N_QKV_TILES)
    def _():
        for m in range(seq // PROJ_ROW_CHUNK):
            rows, res = chunk(m)
            rest_ref[rows, :] = res.astype(BF16)


def _projection(x, norm_gain, scale, shift, w_in_bf16):
    batch, seq, d = x.shape
    n = w_in_bf16.shape[1]
    n_tiles = n // PROJ_COL_TILE
    pairs = PROJ_COL_TILE // LANES
    n_rest = n_tiles - N_QKV_TILES
    padded = seq + ATTN_PAD
    return pl.pallas_call(
        _proj_kernel,
        out_shape=(jax.ShapeDtypeStruct((batch, N_QKV_TILES * pairs, padded, LANES), BF16),
                   jax.ShapeDtypeStruct((batch, seq, n_rest * PROJ_COL_TILE), BF16)),
        grid=(batch, n_tiles),
        in_specs=[pl.BlockSpec((None, seq, d), lambda b, j: (b, 0, 0)),
                  pl.BlockSpec((1, d), lambda b, j: (0, 0)),
                  pl.BlockSpec((None, 1, d), lambda b, j: (b, 0, 0)),
                  pl.BlockSpec((None, 1, d), lambda b, j: (b, 0, 0)),
                  pl.BlockSpec((d, PROJ_COL_TILE), lambda b, j: (0, j))],
        out_specs=(pl.BlockSpec((None, pairs, padded, LANES),
                                lambda b, j: (b, jnp.minimum(j, N_QKV_TILES - 1), 0, 0)),
                   pl.BlockSpec((None, seq, PROJ_COL_TILE),
                                lambda b, j: (b, 0, jnp.maximum(j - N_QKV_TILES, 0)))),
        scratch_shapes=[pltpu.VMEM((seq, d), BF16)],
        compiler_params=pltpu.CompilerParams(dimension_semantics=("arbitrary", "arbitrary"),
                                             vmem_limit_bytes=VMEM_LIMIT),
        name="in_proj",
    )(x, norm_gain.reshape(1, d), scale.reshape(batch, 1, d), shift.reshape(batch, 1, d),
      w_in_bf16)


def _attn_kernel(q_ref, k_ref, v_ref, tri_ref, o_ref):
    pairs, rows, _ = q_ref.shape
    keys = ATTN_KEYS
    half = rows // 2
    assert half == keys and ATTN_PAD == 2 * keys
    tile = 2 * keys
    i = pl.program_id(1)
    first_head = lax.broadcasted_iota(jnp.int32, (1, LANES), 1) < HEAD_DIM
    row = lax.broadcasted_iota(jnp.int32, (half, tile), 0)
    key = lax.broadcasted_iota(jnp.int32, (half, tile), 1) & (keys - 1)
    below_diagonal = key < row
    tri = tri_ref[...]
    first_kb = 2 * i + 2

    def split_heads(x):
        zero = jnp.zeros((keys, LANES), x.dtype)
        parts = []
        for n in range(x.shape[0] // keys):
            blk = x[n * keys:(n + 1) * keys]
            parts += [jnp.where(first_head, blk, zero), jnp.where(first_head, zero, blk)]
        return jnp.concatenate(parts, axis=0)

    def softplus(z):
        return jnp.maximum(z, jnp.log(1.0 + jnp.exp(jnp.minimum(z, EXP_CLAMP))))

    def lane_mass(m0, m1):
        n = m0.shape[0]
        return jnp.concatenate([jnp.broadcast_to(m0, (n, keys)),
                                jnp.broadcast_to(m1, (n, keys))], axis=1)

    def totals(later):
        return later[:, 0:1], later[:, keys:keys + 1]

    def alive(m0, m1):
        return jnp.min(jnp.minimum(m0, m1)) < DEAD_MASS

    no_prev = jnp.where(i == 0, 1e30, 0.0).astype(F32)

    def fast_pair(p, flags):
        q = q_ref[p] * SB_SCALE
        win = pl.ds(pl.multiple_of((first_kb - 2) * keys, keys), 4 * keys)
        kk = split_heads(k_ref[p, win, :])
        vv = split_heads(v_ref[p, win, :])
        z = lax.dot_general(q, kk, (((1,), (1,)), ((), ())), preferred_element_type=F32)
        zt, zb = z[:half], z[half:]
        t = lambda x, n: x[:, n * tile:(n + 1) * tile]
        sp_t = [softplus(t(zt, 0)), softplus(t(zt, 1)),
                jnp.where(below_diagonal, softplus(t(zt, 2)), 0.0)]
        sp_b = [softplus(t(zb, 1)), softplus(t(zb, 2)),
                jnp.where(below_diagonal, softplus(t(zb, 3)), 0.0)]
        stack = jnp.concatenate([sp_t[0], sp_t[1], sp_b[0], sp_t[2], sp_b[1], sp_b[2]], axis=0)
        later = jnp.dot(stack.astype(BF16), tri, preferred_element_type=F32)
        l_t = [later[0:half], later[half:2 * half], later[3 * half:4 * half]]
        l_b = [later[2 * half:3 * half], later[4 * half:5 * half], later[5 * half:6 * half]]
        m2 = totals(l_t[2])
        m1 = tuple(a + b + no_prev for a, b in zip(m2, totals(l_t[1])))
        m0 = tuple(a + b for a, b in zip(m1, totals(l_t[0])))
        w_t = [jnp.exp(t(zt, 0) - l_t[0] - lane_mass(m1[0], m1[1])),
               jnp.exp(t(zt, 1) - l_t[1] - lane_mass(m2[0] + no_prev, m2[1] + no_prev)),
               jnp.where(below_diagonal, jnp.exp(t(zt, 2) - l_t[2]), 0.0),
               jnp.zeros((half, tile), F32)]
        n3 = totals(l_b[2])
        n2 = tuple(a + b for a, b in zip(n3, totals(l_b[1])))
        n1 = tuple(a + b + no_prev for a, b in zip(n2, totals(l_b[0])))
        w_b = [jnp.zeros((half, tile), F32),
               jnp.exp(t(zb, 1) - l_b[0] - lane_mass(n2[0] + no_prev, n2[1] + no_prev)),
               jnp.exp(t(zb, 2) - l_b[1] - lane_mass(n3[0], n3[1])),
               jnp.where(below_diagonal, jnp.exp(t(zb, 3) - l_b[2]), 0.0)]
        w = jnp.concatenate([jnp.concatenate(w_t, axis=1), jnp.concatenate(w_b, axis=1)], axis=0)
        out = jnp.dot(w.astype(BF16), vv, preferred_element_type=F32)
        o_ref[p] = out.astype(o_ref.dtype)
        more = jnp.logical_or(jnp.logical_and(i >= 1, alive(n1[0], n1[1])),
                              jnp.logical_and(i >= 2, alive(m0[0], m0[1])))
        return flags | (more.astype(jnp.int32) << p)

    def fast_pairs(pp, flags):
        return fast_pair(2 * pp + 1, fast_pair(2 * pp, flags))

    flags = lax.fori_loop(0, pairs // 2, fast_pairs, jnp.int32(0))

    def slow_pair(p, _):
        @pl.when(((flags >> p) & 1) == 1)
        def _():
            q = q_ref[p] * SB_SCALE

            def block(kb, state, diagonal_rows):
                acc, m0, m1 = state
                ks = pl.ds(pl.multiple_of(kb * keys, keys), keys)
                kk = split_heads(k_ref[p, ks, :])
                vv = split_heads(v_ref[p, ks, :])

                def mask(x):
                    if diagonal_rows is None:
                        return x
                    lo = diagonal_rows
                    parts = [x[:lo]] if lo else []
                    parts.append(jnp.where(below_diagonal, x[lo:lo + half], 0.0))
                    if lo + half < rows:
                        parts.append(x[lo + half:])
                    return jnp.concatenate(parts, axis=0)

                z = lax.dot_general(q, kk, (((1,), (1,)), ((), ())),
                                    preferred_element_type=F32)
                sp = mask(softplus(z))
                later = jnp.dot(sp.astype(BF16), tri, preferred_element_type=F32)
                w = mask(jnp.exp(z - later - lane_mass(m0, m1)))
                out = jnp.dot(w.astype(BF16), vv, preferred_element_type=F32)
                t0, t1 = totals(later)
                return acc + out, m0 + t0, m1 + t1

            zero = jnp.zeros((half, 1), F32)
            big = jnp.concatenate([jnp.full((half, 1), 1e30, F32), zero], axis=0)
            acc, m0, m1 = block(first_kb + 1, (jnp.zeros((rows, LANES), F32), big, big), half)
            m0 = jnp.concatenate([zero, m0[half:]], axis=0)
            m1 = jnp.concatenate([zero, m1[half:]], axis=0)
            state = block(first_kb, (acc, m0, m1), 0)

            def cond(carry):
                kb, go, _ = carry
                return jnp.logical_and(kb >= ATTN_PAD // keys, go)

            def body(carry):
                kb, _, state = carry
                state = block(kb, state, None)
                return kb - 1, alive(state[1], state[2]), state

            _, _, state = lax.while_loop(cond, body, (first_kb - 1, True, state))
            o_ref[p] = state[0].astype(o_ref.dtype)
        return 0

    @pl.when(flags != 0)
    def _():
        lax.fori_loop(0, pairs, slow_pair, 0)


def _attention(qkv):
    batch, n, padded, _ = qkv.shape
    seq = padded - ATTN_PAD
    pairs = n // N_QKV_TILES
    rows, keys = ATTN_ROWS, ATTN_KEYS
    upper = (jnp.arange(keys)[:, None] >= jnp.arange(keys)[None, :]).astype(BF16)
    zeros = jnp.zeros_like(upper)
    tri = jnp.block([[upper, zeros], [zeros, upper]])
    return pl.pallas_call(
        _attn_kernel,
        out_shape=jax.ShapeDtypeStruct((batch, pairs, seq, LANES), BF16),
        grid=(batch, seq // rows),
        in_specs=[pl.BlockSpec((None, pairs, rows, LANES),
                               lambda b, i: (b, 0, i + ATTN_PAD // ATTN_ROWS, 0)),
                  pl.BlockSpec((None, pairs, padded, LANES), lambda b, i: (b, 1, 0, 0)),
                  pl.BlockSpec((None, pairs, padded, LANES), lambda b, i: (b, 2, 0, 0)),
                  pl.BlockSpec((2 * keys, 2 * keys), lambda b, i: (0, 0))],
        out_specs=pl.BlockSpec((None, pairs, rows, LANES), lambda b, i: (b, 0, i, 0)),
        compiler_params=pltpu.CompilerParams(dimension_semantics=("arbitrary", "arbitrary"),
                                             vmem_limit_bytes=VMEM_LIMIT),
        name="sb_attention",
    )(qkv, qkv, qkv, tri)


def _tail_kernel(attn_ref, za_ref, u_ref, uprev_ref, zb_ref, ma_ref, mb_ref, x_ref, gate_ref,
                 band_ref, poolw_ref, pscale_ref, wa_ref, wb_ref, wo_ref, fgain_ref,
                 o_ref, uext_ref, pooled_ref):
    pairs, rows, _ = attn_ref.shape
    d = x_ref.shape[1]
    group = d // len(POOL_WINDOWS)
    i = pl.program_id(1)

    attn = jnp.concatenate([attn_ref[p] for p in range(pairs)], axis=1)
    a_in = attn.astype(F32) * _silu(za_ref[...].astype(F32))
    p_a = jnp.dot(a_in.astype(BF16), wa_ref[...], preferred_element_type=F32)

    prev = uprev_ref[...]
    uext_ref[0:MAX_WINDOW, :] = jnp.where(i > 0, prev, jnp.zeros_like(prev))
    uext_ref[MAX_WINDOW:, :] = u_ref[...]
    pos = lax.broadcasted_iota(jnp.int32, (rows, 1), 0) + i * rows + 1
    for g, win in enumerate(POOL_WINDOWS):
        cols = slice(g * group, (g + 1) * group)
        wsum = jnp.dot(band_ref[g], uext_ref[:, cols], preferred_element_type=F32)
        inv_count = 1.0 / jnp.minimum(pos, win).astype(F32)
        diff = wsum * inv_count - u_ref[:, cols].astype(F32)
        mixed = jnp.dot(diff.astype(BF16), poolw_ref[g], preferred_element_type=F32)
        pooled_ref[:, cols] = mixed * pscale_ref[:, cols]
    b_in = pooled_ref[...] * _silu(zb_ref[...].astype(F32))
    p_b = jnp.dot(b_in.astype(BF16), wb_ref[...], preferred_element_type=F32)

    merged = (jax.nn.sigmoid(ma_ref[...].astype(F32)) * p_a
              + jax.nn.sigmoid(mb_ref[...].astype(F32)) * p_b)
    out = jnp.dot(merged.astype(BF16), wo_ref[...], preferred_element_type=F32)
    y = x_ref[...] + gate_ref[...] * out
    y = y * lax.rsqrt(jnp.mean(y * y, axis=-1, keepdims=True) + EPS)
    o_ref[...] = y * fgain_ref[...]


def _tail(attn, rest, x, gate, pool_w, pool_scale, w_branch_a, w_branch_b, w_out, final_gain):
    batch, seq, d = x.shape
    pairs = attn.shape[1]
    rows = TAIL_ROWS
    n_win = len(POOL_WINDOWS)
    group = d // n_win
    t = jnp.arange(rows)[:, None] + MAX_WINDOW
    tp = jnp.arange(rows + MAX_WINDOW)[None, :]
    band = jnp.stack([((tp <= t) & (tp > t - w)) for w in POOL_WINDOWS]).astype(BF16)

    tile = lambda col: pl.BlockSpec((None, rows, d), lambda b, i: (b, i, col))
    prev_blocks = rows // MAX_WINDOW
    full2 = lambda shape: pl.BlockSpec(shape, lambda b, i: (0, 0))
    full3 = lambda shape: pl.BlockSpec(shape, lambda b, i: (0, 0, 0))
    return pl.pallas_call(
        _tail_kernel,
        out_shape=jax.ShapeDtypeStruct((batch, seq, d), F32),
        grid=(batch, seq // rows),
        in_specs=[pl.BlockSpec((None, pairs, rows, LANES), lambda b, i: (b, 0, i, 0)),
                  tile(0),
                  tile(1),
                  pl.BlockSpec((None, MAX_WINDOW, d),
                               lambda b, i: (b, jnp.maximum(i * prev_blocks - 1, 0), 1)),
                  tile(2), tile(3), tile(4),
                  tile(0),
                  pl.BlockSpec((None, 1, d), lambda b, i: (b, 0, 0)),
                  full3((n_win, rows, rows + MAX_WINDOW)),
                  full3((n_win, group, group)),
                  full2((1, d)),
                  full2((d, d)), full2((d, d)), full2((d, d)),
                  full2((1, d))],
        out_specs=pl.BlockSpec((None, rows, d), lambda b, i: (b, i, 0)),
        scratch_shapes=[pltpu.VMEM((rows + MAX_WINDOW, d), BF16),
                        pltpu.VMEM((rows, d), F32)],
        compiler_params=pltpu.CompilerParams(dimension_semantics=("arbitrary", "arbitrary"),
                                             vmem_limit_bytes=VMEM_LIMIT),
        name="tail",
    )(attn, rest, rest, rest, rest, rest, rest, x, gate.reshape(batch, 1, d), band,
      pool_w.astype(BF16), pool_scale.reshape(1, d), w_branch_a.astype(BF16),
      w_branch_b.astype(BF16), w_out.astype(BF16), final_gain.reshape(1, d))


def kernel(x, c, norm_gain, w_ada, b_ada, w_in, pool_w, pool_scale, w_branch_a, w_branch_b,
           w_out, final_gain):
    depth = norm_gain.shape[0]
    assert depth == 1, "single-layer block"
    d = x.shape[-1]
    mod = _modulation(c, w_ada[0], b_ada[0])
    shift, scale, gate = mod[:, :d], mod[:, d:2 * d], mod[:, 2 * d:]
    qkv, rest = _projection(x, norm_gain[0], scale, shift, w_in[0].astype(BF16))
    attn = _attention(qkv)
    return _tail(attn, rest, x, gate, pool_w[0], pool_scale[0], w_branch_a[0], w_branch_b[0],
                 w_out[0], final_gain)
```

```python
import jax
import jax.numpy as jnp
from jax import lax
from jax.experimental import pallas as pl
from jax.experimental.pallas import tpu as pltpu

N_HEADS = 16
HEAD_DIM = 64
SB_SCALE = HEAD_DIM ** -0.5
POOL_WINDOWS = (2, 4, 8, 16)
MAX_WINDOW = max(POOL_WINDOWS)
EPS = 1e-6

LANES = 128
PROJ_COL_TILE = 1024
PROJ_ROW_CHUNK = 512
NORM_ROW_CHUNK = 256
ATTN_ROWS = 256
ATTN_KEYS = 128
ATTN_PAD = 256
FAST_PAIRS_PER_STEP = 4
TAIL_ROWS = 512
VMEM_LIMIT = 48 * 1024 * 1024
DEAD_MASS = 104.0
EXP_CLAMP = 88.0

F32 = jnp.float32
BF16 = jnp.bfloat16


def _silu(z):
    return z * jax.nn.sigmoid(z)


def _mod_kernel(c_ref, w_ref, b_ref, o_ref):
    c = c_ref[...]
    o_ref[...] = jnp.dot(_silu(c), w_ref[...], preferred_element_type=F32,
                         precision=lax.Precision.HIGHEST) + b_ref[...]


def _modulation(c, w_ada, b_ada):
    batch, d = c.shape
    n = w_ada.shape[1]
    return pl.pallas_call(
        _mod_kernel,
        out_shape=jax.ShapeDtypeStruct((batch, n), F32),
        grid=(n // d,),
        in_specs=[pl.BlockSpec((batch, d), lambda j: (0, 0)),
                  pl.BlockSpec((d, d), lambda j: (0, j)),
                  pl.BlockSpec((1, d), lambda j: (0, j))],
        out_specs=pl.BlockSpec((batch, d), lambda j: (0, j)),
        compiler_params=pltpu.CompilerParams(dimension_semantics=("arbitrary",),
                                             vmem_limit_bytes=VMEM_LIMIT),
        name="adaln_mod",
    )(c, w_ada, b_ada.reshape(1, n))


N_QKV_TILES = 3


def _proj_kernel(x_ref, gain_ref, scale_ref, shift_ref, w_ref, qkv_ref, rest_ref, h_ref):
    seq = x_ref.shape[0]
    j = pl.program_id(1)
    pairs = qkv_ref.shape[0]

    @pl.when(j == 0)
    def _():
        gain = gain_ref[...]
        mul = 1.0 + scale_ref[...]
        shift = shift_ref[...]

        def body(r, carry):
            rows = pl.ds(pl.multiple_of(r * NORM_ROW_CHUNK, NORM_ROW_CHUNK), NORM_ROW_CHUNK)
            xf = x_ref[rows, :]
            y = xf * lax.rsqrt(jnp.mean(xf * xf, axis=-1, keepdims=True) + EPS)
            h_ref[rows, :] = ((y * gain) * mul + shift).astype(BF16)
            return carry

        lax.fori_loop(0, seq // NORM_ROW_CHUNK, body, 0)

    def chunk(m):
        rows = slice(m * PROJ_ROW_CHUNK, (m + 1) * PROJ_ROW_CHUNK)
        return rows, jnp.dot(h_ref[rows, :], w_ref[...], preferred_element_type=F32)

    @pl.when(j < N_QKV_TILES)
    def _():
        qkv_ref[:, 0:ATTN_PAD, :] = jnp.zeros((pairs, ATTN_PAD, LANES), BF16)
        for m in range(seq // PROJ_ROW_CHUNK):
            rows, res = chunk(m)
            res = res.astype(BF16)
            for p in range(pairs):
                qkv_ref[p, ATTN_PAD + rows.start:ATTN_PAD + rows.stop, :] = (
                    res[:, p * LANES:(p + 1) * LANES])

    @pl.when(j >= N_QKV_TILES)
    def _():
        for m in range(seq // PROJ_ROW_CHUNK):
            rows, res = chunk(m)
            rest_ref[rows, :] = res.astype(BF16)


def _projection(x, norm_gain, scale, shift, w_in_bf16):
    batch, seq, d = x.shape
    n = w_in_bf16.shape[1]
    n_tiles = n // PROJ_COL_TILE
    pairs = PROJ_COL_TILE // LANES
    n_rest = n_tiles - N_QKV_TILES
    padded = seq + ATTN_PAD
    return pl.pallas_call(
        _proj_kernel,
        out_shape=(jax.ShapeDtypeStruct((batch, N_QKV_TILES * pairs, padded, LANES), BF16),
                   jax.ShapeDtypeStruct((batch, seq, n_rest * PROJ_COL_TILE), BF16)),
        grid=(batch, n_tiles),
        in_specs=[pl.BlockSpec((None, seq, d), lambda b, j: (b, 0, 0)),
                  pl.BlockSpec((1, d), lambda b, j: (0, 0)),
                  pl.BlockSpec((None, 1, d), lambda b, j: (b, 0, 0)),
                  pl.BlockSpec((None, 1, d), lambda b, j: (b, 0, 0)),
                  pl.BlockSpec((d, PROJ_COL_TILE), lambda b, j: (0, j))],
        out_specs=(pl.BlockSpec((None, pairs, padded, LANES),
                                lambda b, j: (b, jnp.minimum(j, N_QKV_TILES - 1), 0, 0)),
                   pl.BlockSpec((None, seq, PROJ_COL_TILE),
                                lambda b, j: (b, 0, jnp.maximum(j - N_QKV_TILES, 0)))),
        scratch_shapes=[pltpu.VMEM((seq, d), BF16)],
        compiler_params=pltpu.CompilerParams(dimension_semantics=("arbitrary", "arbitrary"),
                                             vmem_limit_bytes=VMEM_LIMIT),
        name="in_proj",
    )(x, norm_gain.reshape(1, d), scale.reshape(batch, 1, d), shift.reshape(batch, 1, d),
      w_in_bf16)


def _attn_kernel(q_ref, k_ref, v_ref, tri_ref, o_ref):
    pairs, rows, _ = q_ref.shape
    keys = ATTN_KEYS
    half = rows // 2
    assert half == keys and ATTN_PAD == 2 * keys
    tile = 2 * keys
    i = pl.program_id(1)
    first_head = lax.broadcasted_iota(jnp.int32, (1, LANES), 1) < HEAD_DIM
    row = lax.broadcasted_iota(jnp.int32, (half, tile), 0)
    key = lax.broadcasted_iota(jnp.int32, (half, tile), 1) & (keys - 1)
    below_diagonal = key < row
    tri = tri_ref[...]
    first_kb = 2 * i + 2

    def split_heads(x):
        zero = jnp.zeros((keys, LANES), x.dtype)
        parts = []
        for n in range(x.shape[0] // keys):
            blk = x[n * keys:(n + 1) * keys]
            parts += [jnp.where(first_head, blk, zero), jnp.where(first_head, zero, blk)]
        return jnp.concatenate(parts, axis=0)

    def softplus(z):
        return jnp.maximum(z, jnp.log(1.0 + jnp.exp(jnp.minimum(z, EXP_CLAMP))))

    def lane_mass(m0, m1):
        n = m0.shape[0]
        return jnp.concatenate([jnp.broadcast_to(m0, (n, keys)),
                                jnp.broadcast_to(m1, (n, keys))], axis=1)

    def totals(later):
        return later[:, 0:1], later[:, keys:keys + 1]

    def alive(m0, m1):
        return jnp.min(jnp.minimum(m0, m1)) < DEAD_MASS

    no_prev = jnp.where(i == 0, 1e30, 0.0).astype(F32)

    def fast_pair(p, flags):
        q = q_ref[p] * SB_SCALE
        win = pl.ds(pl.multiple_of((first_kb - 2) * keys, keys), 4 * keys)
        kk = split_heads(k_ref[p, win, :])
        vv = split_heads(v_ref[p, win, :])
        z = lax.dot_general(q, kk, (((1,), (1,)), ((), ())), preferred_element_type=F32)
        zt, zb = z[:half], z[half:]
        t = lambda x, n: x[:, n * tile:(n + 1) * tile]
        sp_t = [softplus(t(zt, 0)), softplus(t(zt, 1)),
                jnp.where(below_diagonal, softplus(t(zt, 2)), 0.0)]
        sp_b = [softplus(t(zb, 1)), softplus(t(zb, 2)),
                jnp.where(below_diagonal, softplus(t(zb, 3)), 0.0)]
        stack = jnp.concatenate([sp_t[0], sp_t[1], sp_b[0], sp_t[2], sp_b[1], sp_b[2]], axis=0)
        later = jnp.dot(stack.astype(BF16), tri, preferred_element_type=F32)
        l_t = [later[0:half], later[half:2 * half], later[3 * half:4 * half]]
        l_b = [later[2 * half:3 * half], later[4 * half:5 * half], later[5 * half:6 * half]]
        m2 = totals(l_t[2])
        m1 = tuple(a + b + no_prev for a, b in zip(m2, totals(l_t[1])))
        m0 = tuple(a + b for a, b in zip(m1, totals(l_t[0])))
        w_t = [jnp.exp(t(zt, 0) - l_t[0] - lane_mass(m1[0], m1[1])),
               jnp.exp(t(zt, 1) - l_t[1] - lane_mass(m2[0] + no_prev, m2[1] + no_prev)),
               jnp.where(below_diagonal, jnp.exp(t(zt, 2) - l_t[2]), 0.0),
               jnp.zeros((half, tile), F32)]
        n3 = totals(l_b[2])
        n2 = tuple(a + b for a, b in zip(n3, totals(l_b[1])))
        n1 = tuple(a + b + no_prev for a, b in zip(n2, totals(l_b[0])))
        w_b = [jnp.zeros((half, tile), F32),
               jnp.exp(t(zb, 1) - l_b[0] - lane_mass(n2[0] + no_prev, n2[1] + no_prev)),
               jnp.exp(t(zb, 2) - l_b[1] - lane_mass(n3[0], n3[1])),
               jnp.where(below_diagonal, jnp.exp(t(zb, 3) - l_b[2]), 0.0)]
        w = jnp.concatenate([jnp.concatenate(w_t, axis=1), jnp.concatenate(w_b, axis=1)], axis=0)
        out = jnp.dot(w.astype(BF16), vv, preferred_element_type=F32)
        o_ref[p] = out.astype(o_ref.dtype)
        more = jnp.logical_or(jnp.logical_and(i >= 1, alive(n1[0], n1[1])),
                              jnp.logical_and(i >= 2, alive(m0[0], m0[1])))
        return flags | (more.astype(jnp.int32) << p)

    def fast_pairs(pp, flags):
        for n in range(FAST_PAIRS_PER_STEP):
            flags = fast_pair(FAST_PAIRS_PER_STEP * pp + n, flags)
        return flags

    flags = lax.fori_loop(0, pairs // FAST_PAIRS_PER_STEP, fast_pairs, jnp.int32(0))

    def slow_pair(p, _):
        @pl.when(((flags >> p) & 1) == 1)
        def _():
            q = q_ref[p] * SB_SCALE

            def block(kb, state, diagonal_rows):
                acc, m0, m1 = state
                ks = pl.ds(pl.multiple_of(kb * keys, keys), keys)
                kk = split_heads(k_ref[p, ks, :])
                vv = split_heads(v_ref[p, ks, :])

                def mask(x):
                    if diagonal_rows is None:
                        return x
                    lo = diagonal_rows
                    parts = [x[:lo]] if lo else []
                    parts.append(jnp.where(below_diagonal, x[lo:lo + half], 0.0))
                    if lo + half < rows:
                        parts.append(x[lo + half:])
                    return jnp.concatenate(parts, axis=0)

                z = lax.dot_general(q, kk, (((1,), (1,)), ((), ())),
                                    preferred_element_type=F32)
                sp = mask(softplus(z))
                later = jnp.dot(sp.astype(BF16), tri, preferred_element_type=F32)
                w = mask(jnp.exp(z - later - lane_mass(m0, m1)))
                out = jnp.dot(w.astype(BF16), vv, preferred_element_type=F32)
                t0, t1 = totals(later)
                return acc + out, m0 + t0, m1 + t1

            zero = jnp.zeros((half, 1), F32)
            big = jnp.concatenate([jnp.full((half, 1), 1e30, F32), zero], axis=0)
            acc, m0, m1 = block(first_kb + 1, (jnp.zeros((rows, LANES), F32), big, big), half)
            m0 = jnp.concatenate([zero, m0[half:]], axis=0)
            m1 = jnp.concatenate([zero, m1[half:]], axis=0)
            state = block(first_kb, (acc, m0, m1), 0)

            def cond(carry):
                kb, go, _ = carry
                return jnp.logical_and(kb >= ATTN_PAD // keys, go)

            def body(carry):
                kb, _, state = carry
                state = block(kb - 1, block(kb, state, None), None)
                return kb - 2, alive(state[1], state[2]), state

            _, _, state = lax.while_loop(cond, body, (first_kb - 1, True, state))
            o_ref[p] = state[0].astype(o_ref.dtype)
        return 0

    @pl.when(flags != 0)
    def _():
        lax.fori_loop(0, pairs, slow_pair, 0)


def _attention(qkv):
    batch, n, padded, _ = qkv.shape
    seq = padded - ATTN_PAD
    pairs = n // N_QKV_TILES
    rows, keys = ATTN_ROWS, ATTN_KEYS
    upper = (jnp.arange(keys)[:, None] >= jnp.arange(keys)[None, :]).astype(BF16)
    zeros = jnp.zeros_like(upper)
    tri = jnp.block([[upper, zeros], [zeros, upper]])
    return pl.pallas_call(
        _attn_kernel,
        out_shape=jax.ShapeDtypeStruct((batch, pairs, seq, LANES), BF16),
        grid=(batch, seq // rows),
        in_specs=[pl.BlockSpec((None, pairs, rows, LANES),
                               lambda b, i: (b, 0, i + ATTN_PAD // ATTN_ROWS, 0)),
                  pl.BlockSpec((None, pairs, padded, LANES), lambda b, i: (b, 1, 0, 0)),
                  pl.BlockSpec((None, pairs, padded, LANES), lambda b, i: (b, 2, 0, 0)),
                  pl.BlockSpec((2 * keys, 2 * keys), lambda b, i: (0, 0))],
        out_specs=pl.BlockSpec((None, pairs, rows, LANES), lambda b, i: (b, 0, i, 0)),
        compiler_params=pltpu.CompilerParams(dimension_semantics=("arbitrary", "arbitrary"),
                                             vmem_limit_bytes=VMEM_LIMIT),
        name="sb_attention",
    )(qkv, qkv, qkv, tri)


def _silu_bf16(z):
    h = z * jnp.asarray(0.5, BF16)
    return h + h * jnp.tanh(h)


def _sigmoid_bf16(z):
    half = jnp.asarray(0.5, BF16)
    return half + half * jnp.tanh(z * half)


def _tail_kernel(attn_ref, za_ref, u_ref, uprev_ref, zb_ref, ma_ref, mb_ref, x_ref, gate_ref,
                 poolw_ref, pscale_ref, wa_ref, wb_ref, wo_ref, fgain_ref,
                 o_ref, pooled_ref):
    pairs, rows, _ = attn_ref.shape
    d = x_ref.shape[1]
    group = d // len(POOL_WINDOWS)
    i = pl.program_id(1)

    attn = jnp.concatenate([attn_ref[p] for p in range(pairs)], axis=1)
    a_in = attn * _silu_bf16(za_ref[...])
    p_a = jnp.dot(a_in, wa_ref[...], preferred_element_type=F32)

    prev = uprev_ref[...]
    prev = jnp.where(i > 0, prev, jnp.zeros_like(prev))
    pos = lax.broadcasted_iota(jnp.int32, (rows, 1), 0) + i * rows + 1
    for g, win in enumerate(POOL_WINDOWS):
        cols = slice(g * group, (g + 1) * group)
        u = u_ref[:, cols].astype(F32)
        s = jnp.concatenate([prev[:, cols].astype(F32), u], axis=0)
        shift = 1
        while shift < win:
            s = s + pltpu.roll(s, shift, axis=0)
            shift *= 2
        inv_count = 1.0 / jnp.minimum(pos, win).astype(F32)
        diff = s[MAX_WINDOW:] * inv_count - u
        mixed = jnp.dot(diff.astype(BF16), poolw_ref[g], preferred_element_type=F32)
        pooled_ref[:, cols] = (mixed * pscale_ref[:, cols]).astype(BF16)
    b_in = pooled_ref[...] * _silu_bf16(zb_ref[...])
    p_b = jnp.dot(b_in, wb_ref[...], preferred_element_type=F32)

    merged = (_sigmoid_bf16(ma_ref[...]) * p_a.astype(BF16)
              + _sigmoid_bf16(mb_ref[...]) * p_b.astype(BF16))
    out = jnp.dot(merged, wo_ref[...], preferred_element_type=F32)
    y = x_ref[...] + gate_ref[...] * out
    y = y * lax.rsqrt(jnp.mean(y * y, axis=-1, keepdims=True) + EPS)
    o_ref[...] = y * fgain_ref[...]


def _tail(attn, rest, x, gate, pool_w, pool_scale, w_branch_a, w_branch_b, w_out, final_gain):
    batch, seq, d = x.shape
    pairs = attn.shape[1]
    rows = TAIL_ROWS
    n_win = len(POOL_WINDOWS)
    group = d // n_win
    tile = lambda col: pl.BlockSpec((None, rows, d), lambda b, i: (b, i, col))
    prev_blocks = rows // MAX_WINDOW
    full2 = lambda shape: pl.BlockSpec(shape, lambda b, i: (0, 0))
    full3 = lambda shape: pl.BlockSpec(shape, lambda b, i: (0, 0, 0))
    return pl.pallas_call(
        _tail_kernel,
        out_shape=jax.ShapeDtypeStruct((batch, seq, d), F32),
        grid=(batch, seq // rows),
        in_specs=[pl.BlockSpec((None, pairs, rows, LANES), lambda b, i: (b, 0, i, 0)),
                  tile(0),
                  tile(1),
                  pl.BlockSpec((None, MAX_WINDOW, d),
                               lambda b, i: (b, jnp.maximum(i * prev_blocks - 1, 0), 1)),
                  tile(2), tile(3), tile(4),
                  tile(0),
                  pl.BlockSpec((None, 1, d), lambda b, i: (b, 0, 0)),
                  full3((n_win, group, group)),
                  full2((1, d)),
                  full2((d, d)), full2((d, d)), full2((d, d)),
                  full2((1, d))],
        out_specs=pl.BlockSpec((None, rows, d), lambda b, i: (b, i, 0)),
        scratch_shapes=[pltpu.VMEM((rows, d), BF16)],
        compiler_params=pltpu.CompilerParams(dimension_semantics=("arbitrary", "arbitrary"),
                                             vmem_limit_bytes=VMEM_LIMIT),
        name="tail",
    )(attn, rest, rest, rest, rest, rest, rest, x, gate.reshape(batch, 1, d),
      pool_w.astype(BF16), pool_scale.reshape(1, d), w_branch_a.astype(BF16),
      w_branch_b.astype(BF16), w_out.astype(BF16), final_gain.reshape(1, d))


def kernel(x, c, norm_gain, w_ada, b_ada, w_in, pool_w, pool_scale, w_branch_a, w_branch_b,
           w_out, final_gain):
    depth = norm_gain.shape[0]
    assert depth == 1, "single-layer block"
    d = x.shape[-1]
    mod = _modulation(c, w_ada[0], b_ada[0])
    shift, scale, gate = mod[:, :d], mod[:, d:2 * d], mod[:, 2 * d:]
    qkv, rest = _projection(x, norm_gain[0], scale, shift, w_in[0].astype(BF16))
    attn = _attention(qkv)
    return _tail(attn, rest, x, gate, pool_w[0], pool_scale[0], w_branch_a[0], w_branch_b[0],
                 w_out[0], final_gain)
```

```python
import jax
import jax.numpy as jnp
from jax import lax
from jax.experimental import pallas as pl
from jax.experimental.pallas import tpu as pltpu

N_HEADS = 16
HEAD_DIM = 64
SB_SCALE = HEAD_DIM ** -0.5
POOL_WINDOWS = (2, 4, 8, 16)
MAX_WINDOW = max(POOL_WINDOWS)
EPS = 1e-6

LANES = 128
PROJ_COL_TILE = 1024
PROJ_ROW_CHUNK = 512
NORM_ROW_CHUNK = 256
ATTN_ROWS = 256
ATTN_KEYS = 128
ATTN_PAD = 256
FAST_PAIRS_PER_STEP = 8
TAIL_ROWS = 512
VMEM_LIMIT = 48 * 1024 * 1024
DEAD_MASS = 104.0
EXP_CLAMP = 88.0

F32 = jnp.float32
BF16 = jnp.bfloat16


def _silu(z):
    return z * jax.nn.sigmoid(z)


def _mod_kernel(c_ref, w_ref, b_ref, o_ref):
    c = c_ref[...]
    o_ref[...] = jnp.dot(_silu(c), w_ref[...], preferred_element_type=F32,
                         precision=lax.Precision.HIGHEST) + b_ref[...]


def _modulation(c, w_ada, b_ada):
    batch, d = c.shape
    n = w_ada.shape[1]
    return pl.pallas_call(
        _mod_kernel,
        out_shape=jax.ShapeDtypeStruct((batch, n), F32),
        grid=(n // d,),
        in_specs=[pl.BlockSpec((batch, d), lambda j: (0, 0)),
                  pl.BlockSpec((d, d), lambda j: (0, j)),
                  pl.BlockSpec((1, d), lambda j: (0, j))],
        out_specs=pl.BlockSpec((batch, d), lambda j: (0, j)),
        compiler_params=pltpu.CompilerParams(dimension_semantics=("arbitrary",),
                                             vmem_limit_bytes=VMEM_LIMIT),
        name="adaln_mod",
    )(c, w_ada, b_ada.reshape(1, n))


N_QKV_TILES = 3


def _proj_kernel(x_ref, gain_ref, scale_ref, shift_ref, w_ref, qkv_ref, rest_ref, h_ref):
    seq = x_ref.shape[0]
    j = pl.program_id(1)
    pairs = qkv_ref.shape[0]

    def norm_rows(rows):
        xf = x_ref[rows, :]
        y = xf * lax.rsqrt(jnp.mean(xf * xf, axis=-1, keepdims=True) + EPS)
        return ((y * gain_ref[...]) * (1.0 + scale_ref[...]) + shift_ref[...]).astype(BF16)

    def chunk(m):
        rows = slice(m * PROJ_ROW_CHUNK, (m + 1) * PROJ_ROW_CHUNK)
        return rows, jnp.dot(h_ref[rows, :], w_ref[...], preferred_element_type=F32)

    def store_qkv(rows, res):
        res = res.astype(BF16)
        for p in range(pairs):
            qkv_ref[p, ATTN_PAD + rows.start:ATTN_PAD + rows.stop, :] = (
                res[:, p * LANES:(p + 1) * LANES])

    zero_pad = jnp.zeros((pairs, ATTN_PAD, LANES), BF16)

    @pl.when(j == 0)
    def _():
        qkv_ref[:, 0:ATTN_PAD, :] = zero_pad
        for m in range(seq // PROJ_ROW_CHUNK):
            rows = slice(m * PROJ_ROW_CHUNK, (m + 1) * PROJ_ROW_CHUNK)
            for r in range(rows.start, rows.stop, NORM_ROW_CHUNK):
                sub = slice(r, r + NORM_ROW_CHUNK)
                h_ref[sub, :] = norm_rows(sub)
            store_qkv(*chunk(m))

    @pl.when(jnp.logical_and(j > 0, j < N_QKV_TILES))
    def _():
        qkv_ref[:, 0:ATTN_PAD, :] = zero_pad
        for m in range(seq // PROJ_ROW_CHUNK):
            store_qkv(*chunk(m))

    @pl.when(j >= N_QKV_TILES)
    def _():
        for m in range(seq // PROJ_ROW_CHUNK):
            rows, res = chunk(m)
            rest_ref[rows, :] = res.astype(BF16)


def _projection(x, norm_gain, scale, shift, w_in_bf16):
    batch, seq, d = x.shape
    n = w_in_bf16.shape[1]
    n_tiles = n // PROJ_COL_TILE
    pairs = PROJ_COL_TILE // LANES
    n_rest = n_tiles - N_QKV_TILES
    padded = seq + ATTN_PAD
    return pl.pallas_call(
        _proj_kernel,
        out_shape=(jax.ShapeDtypeStruct((batch, N_QKV_TILES * pairs, padded, LANES), BF16),
                   jax.ShapeDtypeStruct((batch, seq, n_rest * PROJ_COL_TILE), BF16)),
        grid=(batch, n_tiles),
        in_specs=[pl.BlockSpec((None, seq, d), lambda b, j: (b, 0, 0)),
                  pl.BlockSpec((1, d), lambda b, j: (0, 0)),
                  pl.BlockSpec((None, 1, d), lambda b, j: (b, 0, 0)),
                  pl.BlockSpec((None, 1, d), lambda b, j: (b, 0, 0)),
                  pl.BlockSpec((d, PROJ_COL_TILE), lambda b, j: (0, j))],
        out_specs=(pl.BlockSpec((None, pairs, padded, LANES),
                                lambda b, j: (b, jnp.minimum(j, N_QKV_TILES - 1), 0, 0)),
                   pl.BlockSpec((None, seq, PROJ_COL_TILE),
                                lambda b, j: (b, 0, jnp.maximum(j - N_QKV_TILES, 0)))),
        scratch_shapes=[pltpu.VMEM((seq, d), BF16)],
        compiler_params=pltpu.CompilerParams(dimension_semantics=("arbitrary", "arbitrary"),
                                             vmem_limit_bytes=VMEM_LIMIT),
        name="in_proj",
    )(x, norm_gain.reshape(1, d), scale.reshape(batch, 1, d), shift.reshape(batch, 1, d),
      w_in_bf16)


def _attn_kernel(q_ref, k_ref, v_ref, tri_ref, o_ref):
    pairs, rows, _ = q_ref.shape
    keys = ATTN_KEYS
    half = rows // 2
    assert half == keys and ATTN_PAD == 2 * keys
    tile = 2 * keys
    i = pl.program_id(1)
    first_head = lax.broadcasted_iota(jnp.int32, (1, LANES), 1) < HEAD_DIM
    row = lax.broadcasted_iota(jnp.int32, (half, tile), 0)
    key = lax.broadcasted_iota(jnp.int32, (half, tile), 1) & (keys - 1)
    below_diagonal = key < row
    tri = tri_ref[...]
    first_kb = 2 * i + 2

    def split_heads(x):
        zero = jnp.zeros((keys, LANES), x.dtype)
        parts = []
        for n in range(x.shape[0] // keys):
            blk = x[n * keys:(n + 1) * keys]
            parts += [jnp.where(first_head, blk, zero), jnp.where(first_head, zero, blk)]
        return jnp.concatenate(parts, axis=0)

    def softplus(z):
        return jnp.maximum(z, jnp.log(1.0 + jnp.exp(jnp.minimum(z, EXP_CLAMP))))

    def lane_mass(m0, m1):
        n = m0.shape[0]
        return jnp.concatenate([jnp.broadcast_to(m0, (n, keys)),
                                jnp.broadcast_to(m1, (n, keys))], axis=1)

    def totals(later):
        return later[:, 0:1], later[:, keys:keys + 1]

    def alive(m0, m1):
        return jnp.min(jnp.minimum(m0, m1)) < DEAD_MASS

    def fast_pair(p, flags):
        q = q_ref[p] * SB_SCALE
        win = pl.ds(pl.multiple_of((first_kb - 2) * keys, keys), 4 * keys)
        kk = split_heads(k_ref[p, win, :])
        vv = split_heads(v_ref[p, win, :])
        z = lax.dot_general(q, kk, (((1,), (1,)), ((), ())), preferred_element_type=F32)
        zt, zb = z[:half], z[half:]
        t = lambda x, n: x[:, n * tile:(n + 1) * tile]
        sp_t = [softplus(t(zt, 0)), softplus(t(zt, 1)),
                jnp.where(below_diagonal, softplus(t(zt, 2)), 0.0)]
        sp_b = [softplus(t(zb, 1)), softplus(t(zb, 2)),
                jnp.where(below_diagonal, softplus(t(zb, 3)), 0.0)]
        later_t = jnp.dot(jnp.concatenate(sp_t, axis=0).astype(BF16), tri,
                          preferred_element_type=F32)
        later_b = jnp.dot(jnp.concatenate(sp_b, axis=0).astype(BF16), tri,
                          preferred_element_type=F32)
        l_t = [later_t[n * half:(n + 1) * half] for n in range(3)]
        l_b = [later_b[n * half:(n + 1) * half] for n in range(3)]
        m2 = totals(l_t[2])
        m1 = tuple(a + b for a, b in zip(m2, totals(l_t[1])))
        m0 = tuple(a + b for a, b in zip(m1, totals(l_t[0])))
        w_t = [jnp.exp(t(zt, 0) - l_t[0] - lane_mass(m1[0], m1[1])),
               jnp.exp(t(zt, 1) - l_t[1] - lane_mass(m2[0], m2[1])),
               jnp.where(below_diagonal, jnp.exp(t(zt, 2) - l_t[2]), 0.0)]
        n3 = totals(l_b[2])
        n2 = tuple(a + b for a, b in zip(n3, totals(l_b[1])))
        n1 = tuple(a + b for a, b in zip(n2, totals(l_b[0])))
        w_b = [jnp.exp(t(zb, 1) - l_b[0] - lane_mass(n2[0], n2[1])),
               jnp.exp(t(zb, 2) - l_b[1] - lane_mass(n3[0], n3[1])),
               jnp.where(below_diagonal, jnp.exp(t(zb, 3) - l_b[2]), 0.0)]
        out_t = jnp.dot(jnp.concatenate(w_t, axis=1).astype(BF16), vv[:3 * tile],
                        preferred_element_type=F32)
        out_b = jnp.dot(jnp.concatenate(w_b, axis=1).astype(BF16), vv[tile:],
                        preferred_element_type=F32)
        o_ref[p, :half, :] = out_t.astype(o_ref.dtype)
        o_ref[p, half:, :] = out_b.astype(o_ref.dtype)
        more = jnp.logical_or(jnp.logical_and(i >= 1, alive(n1[0], n1[1])),
                              jnp.logical_and(i >= 2, alive(m0[0], m0[1])))
        return flags | (more.astype(jnp.int32) << p)

    def fast_pairs(pp, flags):
        for n in range(FAST_PAIRS_PER_STEP):
            flags = fast_pair(FAST_PAIRS_PER_STEP * pp + n, flags)
        return flags

    flags = lax.fori_loop(0, pairs // FAST_PAIRS_PER_STEP, fast_pairs, jnp.int32(0))

    def slow_pair(p, _):
        @pl.when(((flags >> p) & 1) == 1)
        def _():
            q = q_ref[p] * SB_SCALE

            def block(kb, state, diagonal_rows):
                acc, m0, m1 = state
                ks = pl.ds(pl.multiple_of(kb * keys, keys), keys)
                kk = split_heads(k_ref[p, ks, :])
                vv = split_heads(v_ref[p, ks, :])

                def mask(x):
                    if diagonal_rows is None:
                        return x
                    lo = diagonal_rows
                    parts = [x[:lo]] if lo else []
                    parts.append(jnp.where(below_diagonal, x[lo:lo + half], 0.0))
                    if lo + half < rows:
                        parts.append(x[lo + half:])
                    return jnp.concatenate(parts, axis=0)

                z = lax.dot_general(q, kk, (((1,), (1,)), ((), ())),
                                    preferred_element_type=F32)
                sp = mask(softplus(z))
                later = jnp.dot(sp.astype(BF16), tri, preferred_element_type=F32)
                w = mask(jnp.exp(z - later - lane_mass(m0, m1)))
                out = jnp.dot(w.astype(BF16), vv, preferred_element_type=F32)
                t0, t1 = totals(later)
                return acc + out, m0 + t0, m1 + t1

            zero = jnp.zeros((half, 1), F32)
            big = jnp.concatenate([jnp.full((half, 1), 1e30, F32), zero], axis=0)
            acc, m0, m1 = block(first_kb + 1, (jnp.zeros((rows, LANES), F32), big, big), half)
            m0 = jnp.concatenate([zero, m0[half:]], axis=0)
            m1 = jnp.concatenate([zero, m1[half:]], axis=0)
            state = block(first_kb, (acc, m0, m1), 0)

            def cond(carry):
                kb, go, _ = carry
                return jnp.logical_and(kb >= ATTN_PAD // keys, go)

            def body(carry):
                kb, _, state = carry
                state = block(kb - 1, block(kb, state, None), None)
                return kb - 2, alive(state[1], state[2]), state

            _, _, state = lax.while_loop(cond, body, (first_kb - 1, True, state))
            o_ref[p] = state[0].astype(o_ref.dtype)
        return 0

    @pl.when(flags != 0)
    def _():
        lax.fori_loop(0, pairs, slow_pair, 0)


def _attention(qkv):
    batch, n, padded, _ = qkv.shape
    seq = padded - ATTN_PAD
    pairs = n // N_QKV_TILES
    rows, keys = ATTN_ROWS, ATTN_KEYS
    upper = (jnp.arange(keys)[:, None] >= jnp.arange(keys)[None, :]).astype(BF16)
    zeros = jnp.zeros_like(upper)
    tri = jnp.block([[upper, zeros], [zeros, upper]])
    return pl.pallas_call(
        _attn_kernel,
        out_shape=jax.ShapeDtypeStruct((batch, pairs, seq, LANES), BF16),
        grid=(batch, seq // rows),
        in_specs=[pl.BlockSpec((None, pairs, rows, LANES),
                               lambda b, i: (b, 0, i + ATTN_PAD // ATTN_ROWS, 0)),
                  pl.BlockSpec((None, pairs, padded, LANES), lambda b, i: (b, 1, 0, 0)),
                  pl.BlockSpec((None, pairs, padded, LANES), lambda b, i: (b, 2, 0, 0)),
                  pl.BlockSpec((2 * keys, 2 * keys), lambda b, i: (0, 0))],
        out_specs=pl.BlockSpec((None, pairs, rows, LANES), lambda b, i: (b, 0, i, 0)),
        compiler_params=pltpu.CompilerParams(dimension_semantics=("arbitrary", "arbitrary"),
                                             vmem_limit_bytes=VMEM_LIMIT),
        name="sb_attention",
    )(qkv, qkv, qkv, tri)


def _silu_bf16(z):
    h = z * jnp.asarray(0.5, BF16)
    return h + h * jnp.tanh(h)


def _sigmoid_bf16(z):
    half = jnp.asarray(0.5, BF16)
    return half + half * jnp.tanh(z * half)


def _tail_kernel(attn_ref, za_ref, u_ref, uprev_ref, zb_ref, ma_ref, mb_ref, x_ref, gate_ref,
                 poolw_ref, pscale_ref, wa_ref, wb_ref, wo_ref, fgain_ref,
                 o_ref, pooled_ref):
    pairs, rows, _ = attn_ref.shape
    d = x_ref.shape[1]
    group = d // len(POOL_WINDOWS)
    i = pl.program_id(1)

    attn = jnp.concatenate([attn_ref[p] for p in range(pairs)], axis=1)
    a_in = attn * _silu_bf16(za_ref[...])
    p_a = jnp.dot(a_in, wa_ref[...], preferred_element_type=F32)

    prev = uprev_ref[...]
    prev = jnp.where(i > 0, prev, jnp.zeros_like(prev))
    pos = lax.broadcasted_iota(jnp.int32, (rows, 1), 0) + i * rows + 1
    for g, win in enumerate(POOL_WINDOWS):
        cols = slice(g * group, (g + 1) * group)
        u = u_ref[:, cols].astype(F32)
        s = jnp.concatenate([prev[:, cols].astype(F32), u], axis=0)
        shift = 1
        while shift < win:
            s = s + pltpu.roll(s, shift, axis=0)
            shift *= 2
        inv_count = 1.0 / jnp.minimum(pos, win).astype(F32)
        diff = s[MAX_WINDOW:] * inv_count - u
        mixed = jnp.dot(diff.astype(BF16), poolw_ref[g], preferred_element_type=F32)
        pooled_ref[:, cols] = (mixed * pscale_ref[:, cols]).astype(BF16)
    b_in = pooled_ref[...] * _silu_bf16(zb_ref[...])
    p_b = jnp.dot(b_in, wb_ref[...], preferred_element_type=F32)

    merged = (_sigmoid_bf16(ma_ref[...]) * p_a.astype(BF16)
              + _sigmoid_bf16(mb_ref[...]) * p_b.astype(BF16))
    out = jnp.dot(merged, wo_ref[...], preferred_element_type=F32)
    y = x_ref[...] + gate_ref[...] * out
    y = y * lax.rsqrt(jnp.mean(y * y, axis=-1, keepdims=True) + EPS)
    o_ref[...] = y * fgain_ref[...]


def _tail(attn, rest, x, gate, pool_w, pool_scale, w_branch_a, w_branch_b, w_out, final_gain):
    batch, seq, d = x.shape
    pairs = attn.shape[1]
    rows = TAIL_ROWS
    n_win = len(POOL_WINDOWS)
    group = d // n_win
    tile = lambda col: pl.BlockSpec((None, rows, d), lambda b, i: (b, i, col))
    prev_blocks = rows // MAX_WINDOW
    full2 = lambda shape: pl.BlockSpec(shape, lambda b, i: (0, 0))
    full3 = lambda shape: pl.BlockSpec(shape, lambda b, i: (0, 0, 0))
    return pl.pallas_call(
        _tail_kernel,
        out_shape=jax.ShapeDtypeStruct((batch, seq, d), F32),
        grid=(batch, seq // rows),
        in_specs=[pl.BlockSpec((None, pairs, rows, LANES), lambda b, i: (b, 0, i, 0)),
                  tile(0),
                  tile(1),
                  pl.BlockSpec((None, MAX_WINDOW, d),
                               lambda b, i: (b, jnp.maximum(i * prev_blocks - 1, 0), 1)),
                  tile(2), tile(3), tile(4),
                  tile(0),
                  pl.BlockSpec((None, 1, d), lambda b, i: (b, 0, 0)),
                  full3((n_win, group, group)),
                  full2((1, d)),
                  full2((d, d)), full2((d, d)), full2((d, d)),
                  full2((1, d))],
        out_specs=pl.BlockSpec((None, rows, d), lambda b, i: (b, i, 0)),
        scratch_shapes=[pltpu.VMEM((rows, d), BF16)],
        compiler_params=pltpu.CompilerParams(dimension_semantics=("arbitrary", "arbitrary"),
                                             vmem_limit_bytes=VMEM_LIMIT),
        name="tail",
    )(attn, rest, rest, rest, rest, rest, rest, x, gate.reshape(batch, 1, d),
      pool_w.astype(BF16), pool_scale.reshape(1, d), w_branch_a.astype(BF16),
      w_branch_b.astype(BF16), w_out.astype(BF16), final_gain.reshape(1, d))


def kernel(x, c, norm_gain, w_ada, b_ada, w_in, pool_w, pool_scale, w_branch_a, w_branch_b,
           w_out, final_gain):
    depth = norm_gain.shape[0]
    assert depth == 1, "single-layer block"
    d = x.shape[-1]
    mod = _modulation(c, w_ada[0], b_ada[0])
    shift, scale, gate = mod[:, :d], mod[:, d:2 * d], mod[:, 2 * d:]
    qkv, rest = _projection(x, norm_gain[0], scale, shift, w_in[0].astype(BF16))
    attn = _attention(qkv)
    return _tail(attn, rest, x, gate, pool_w[0], pool_scale[0], w_branch_a[0], w_branch_b[0],
                 w_out[0], final_gain)
```

```python
import jax
import jax.numpy as jnp
from jax import lax
from jax.experimental import pallas as pl
from jax.experimental.pallas import tpu as pltpu

N_HEADS = 16
HEAD_DIM = 64
SB_SCALE = HEAD_DIM ** -0.5
POOL_WINDOWS = (2, 4, 8, 16)
MAX_WINDOW = max(POOL_WINDOWS)
EPS = 1e-6

LANES = 128
PROJ_COL_TILE = 1024
PROJ_ROW_CHUNK = 512
NORM_ROW_CHUNK = 256
ATTN_ROWS = 256
ATTN_STEP_ROWS = 512
ATTN_KEYS = 128
ATTN_PAD = 512
TAIL_ROWS = 512
VMEM_LIMIT = 48 * 1024 * 1024
DEAD_MASS = 104.0
EXP_CLAMP = 88.0

F32 = jnp.float32
BF16 = jnp.bfloat16


def _silu(z):
    return z * jax.nn.sigmoid(z)


def _mod_kernel(c_ref, w_ref, b_ref, o_ref):
    c = c_ref[...]
    o_ref[...] = jnp.dot(_silu(c), w_ref[...], preferred_element_type=F32,
                         precision=lax.Precision.HIGHEST) + b_ref[...]


def _modulation(c, w_ada, b_ada):
    batch, d = c.shape
    n = w_ada.shape[1]
    return pl.pallas_call(
        _mod_kernel,
        out_shape=jax.ShapeDtypeStruct((batch, n), F32),
        grid=(n // d,),
        in_specs=[pl.BlockSpec((batch, d), lambda j: (0, 0)),
                  pl.BlockSpec((d, d), lambda j: (0, j)),
                  pl.BlockSpec((1, d), lambda j: (0, j))],
        out_specs=pl.BlockSpec((batch, d), lambda j: (0, j)),
        compiler_params=pltpu.CompilerParams(dimension_semantics=("arbitrary",),
                                             vmem_limit_bytes=VMEM_LIMIT),
        name="adaln_mod",
    )(c, w_ada, b_ada.reshape(1, n))


N_QKV_TILES = 3
POOL_TILE = 4


def _proj_kernel(x_ref, gain_ref, scale_ref, shift_ref, w_ref, qkv_ref, rest_ref, h_ref):
    seq = x_ref.shape[0]
    j = pl.program_id(1)
    pairs = qkv_ref.shape[0]

    def norm_rows(rows):
        xf = x_ref[rows, :]
        y = xf * lax.rsqrt(jnp.mean(xf * xf, axis=-1, keepdims=True) + EPS)
        return ((y * gain_ref[...]) * (1.0 + scale_ref[...]) + shift_ref[...]).astype(BF16)

    def chunk(m):
        rows = slice(m * PROJ_ROW_CHUNK, (m + 1) * PROJ_ROW_CHUNK)
        return rows, jnp.dot(h_ref[rows, :], w_ref[...], preferred_element_type=F32)

    def store_qkv(rows, res):
        res = res.astype(BF16)
        for p in range(pairs):
            qkv_ref[p, ATTN_PAD + rows.start:ATTN_PAD + rows.stop, :] = (
                res[:, p * LANES:(p + 1) * LANES])

    zero_pad = jnp.zeros((pairs, ATTN_PAD, LANES), BF16)

    @pl.when(j == 0)
    def _():
        qkv_ref[:, 0:ATTN_PAD, :] = zero_pad
        for m in range(seq // PROJ_ROW_CHUNK):
            rows = slice(m * PROJ_ROW_CHUNK, (m + 1) * PROJ_ROW_CHUNK)
            for r in range(rows.start, rows.stop, NORM_ROW_CHUNK):
                sub = slice(r, r + NORM_ROW_CHUNK)
                h_ref[sub, :] = norm_rows(sub)
            store_qkv(*chunk(m))

    @pl.when(jnp.logical_and(j > 0, j < N_QKV_TILES))
    def _():
        qkv_ref[:, 0:ATTN_PAD, :] = zero_pad
        for m in range(seq // PROJ_ROW_CHUNK):
            store_qkv(*chunk(m))

    @pl.when(jnp.logical_and(j >= N_QKV_TILES, j != POOL_TILE))
    def _():
        for m in range(seq // PROJ_ROW_CHUNK):
            rows, res = chunk(m)
            rest_ref[rows, :] = res.astype(BF16)

    @pl.when(j == POOL_TILE)
    def _():
        group = w_ref.shape[1] // len(POOL_WINDOWS)

        def pool(rows, u, prev):
            pos = lax.broadcasted_iota(jnp.int32, (PROJ_ROW_CHUNK, 1), 0) + (rows.start + 1)
            for g, win in enumerate(POOL_WINDOWS):
                cols = slice(g * group, (g + 1) * group)
                s = jnp.concatenate([prev[:, cols], u[:, cols]], axis=0)
                shift = 1
                while shift < win:
                    s = s + pltpu.roll(s, shift, axis=0)
                    shift *= 2
                inv_count = 1.0 / jnp.minimum(pos, win).astype(F32)
                rest_ref[rows, cols] = (s[MAX_WINDOW:] * inv_count - u[:, cols]).astype(BF16)
            return u[PROJ_ROW_CHUNK - MAX_WINDOW:]

        prev = jnp.zeros((MAX_WINDOW, w_ref.shape[1]), F32)
        pending = chunk(0)
        for m in range(1, seq // PROJ_ROW_CHUNK):
            following = chunk(m)
            prev = pool(*pending, prev)
            pending = following
        pool(*pending, prev)


def _projection(x, norm_gain, scale, shift, w_in_bf16):
    batch, seq, d = x.shape
    n = w_in_bf16.shape[1]
    n_tiles = n // PROJ_COL_TILE
    pairs = PROJ_COL_TILE // LANES
    n_rest = n_tiles - N_QKV_TILES
    padded = seq + ATTN_PAD
    return pl.pallas_call(
        _proj_kernel,
        out_shape=(jax.ShapeDtypeStruct((batch, N_QKV_TILES * pairs, padded, LANES), BF16),
                   jax.ShapeDtypeStruct((batch, seq, n_rest * PROJ_COL_TILE), BF16)),
        grid=(batch, n_tiles),
        in_specs=[pl.BlockSpec((None, seq, d), lambda b, j: (b, 0, 0)),
                  pl.BlockSpec((1, d), lambda b, j: (0, 0)),
                  pl.BlockSpec((None, 1, d), lambda b, j: (b, 0, 0)),
                  pl.BlockSpec((None, 1, d), lambda b, j: (b, 0, 0)),
                  pl.BlockSpec((d, PROJ_COL_TILE), lambda b, j: (0, j))],
        out_specs=(pl.BlockSpec((None, pairs, padded, LANES),
                                lambda b, j: (b, jnp.minimum(j, N_QKV_TILES - 1), 0, 0)),
                   pl.BlockSpec((None, seq, PROJ_COL_TILE),
                                lambda b, j: (b, 0, jnp.maximum(j - N_QKV_TILES, 0)))),
        scratch_shapes=[pltpu.VMEM((seq, d), BF16)],
        compiler_params=pltpu.CompilerParams(dimension_semantics=("arbitrary", "arbitrary"),
                                             vmem_limit_bytes=VMEM_LIMIT),
        name="in_proj",
    )(x, norm_gain.reshape(1, d), scale.reshape(batch, 1, d), shift.reshape(batch, 1, d),
      w_in_bf16)


def _attn_kernel(q_ref, k_ref, v_ref, tri_ref, o_ref):
    pairs, step_rows, _ = q_ref.shape
    rows = ATTN_ROWS
    n_sub = step_rows // rows
    keys = ATTN_KEYS
    half = rows // 2
    assert half == keys and ATTN_PAD >= 2 * keys and ATTN_PAD % step_rows == 0
    tile = 2 * keys
    step = pl.program_id(1)
    first_head = lax.broadcasted_iota(jnp.int32, (1, LANES), 1) < HEAD_DIM
    row = lax.broadcasted_iota(jnp.int32, (half, tile), 0)
    key = lax.broadcasted_iota(jnp.int32, (half, tile), 1) & (keys - 1)
    below_diagonal = key < row
    tri = tri_ref[...]

    def first_block(i):
        return 2 * i + ATTN_PAD // keys

    def split_heads(x):
        zero = jnp.zeros((keys, LANES), x.dtype)
        parts = []
        for n in range(x.shape[0] // keys):
            blk = x[n * keys:(n + 1) * keys]
            parts += [jnp.where(first_head, blk, zero), jnp.where(first_head, zero, blk)]
        return jnp.concatenate(parts, axis=0)

    def softplus(z):
        return jnp.maximum(z, jnp.log(1.0 + jnp.exp(jnp.minimum(z, EXP_CLAMP))))

    def lane_mass(m0, m1):
        n = m0.shape[0]
        return jnp.concatenate([jnp.broadcast_to(m0, (n, keys)),
                                jnp.broadcast_to(m1, (n, keys))], axis=1)

    def totals(later):
        return later[:, 0:1], later[:, keys:keys + 1]

    def alive(m0, m1):
        return jnp.min(jnp.minimum(m0, m1)) < DEAD_MASS

    def fast_pair(p, sub, flags):
        i = n_sub * step + sub
        first_kb = first_block(i)
        q = q_ref[p, sub * rows:(sub + 1) * rows, :] * SB_SCALE
        win = pl.ds(pl.multiple_of((first_kb - 2) * keys, keys), 4 * keys)
        kk = split_heads(k_ref[p, win, :])
        vv = split_heads(v_ref[p, win, :])
        z = lax.dot_general(q, kk, (((1,), (1,)), ((), ())), preferred_element_type=F32)
        zt, zb = z[:half], z[half:]
        t = lambda x, n: x[:, n * tile:(n + 1) * tile]
        sp_t = [softplus(t(zt, 0)), softplus(t(zt, 1)),
                jnp.where(below_diagonal, softplus(t(zt, 2)), 0.0)]
        sp_b = [softplus(t(zb, 1)), softplus(t(zb, 2)),
                jnp.where(below_diagonal, softplus(t(zb, 3)), 0.0)]
        later_t = jnp.dot(jnp.concatenate(sp_t, axis=0).astype(BF16), tri,
                          preferred_element_type=F32)
        later_b = jnp.dot(jnp.concatenate(sp_b, axis=0).astype(BF16), tri,
                          preferred_element_type=F32)
        l_t = [later_t[n * half:(n + 1) * half] for n in range(3)]
        l_b = [later_b[n * half:(n + 1) * half] for n in range(3)]
        m2 = totals(l_t[2])
        m1 = tuple(a + b for a, b in zip(m2, totals(l_t[1])))
        m0 = tuple(a + b for a, b in zip(m1, totals(l_t[0])))
        w_t = [jnp.exp(t(zt, 0) - l_t[0] - lane_mass(m1[0], m1[1])),
               jnp.exp(t(zt, 1) - l_t[1] - lane_mass(m2[0], m2[1])),
               jnp.where(below_diagonal, jnp.exp(t(zt, 2) - l_t[2]), 0.0)]
        n3 = totals(l_b[2])
        n2 = tuple(a + b for a, b in zip(n3, totals(l_b[1])))
        n1 = tuple(a + b for a, b in zip(n2, totals(l_b[0])))
        w_b = [jnp.exp(t(zb, 1) - l_b[0] - lane_mass(n2[0], n2[1])),
               jnp.exp(t(zb, 2) - l_b[1] - lane_mass(n3[0], n3[1])),
               jnp.where(below_diagonal, jnp.exp(t(zb, 3) - l_b[2]), 0.0)]
        out_t = jnp.dot(jnp.concatenate(w_t, axis=1).astype(BF16), vv[:3 * tile],
                        preferred_element_type=F32)
        out_b = jnp.dot(jnp.concatenate(w_b, axis=1).astype(BF16), vv[tile:],
                        preferred_element_type=F32)
        o_ref[p, sub * rows:sub * rows + half, :] = out_t.astype(o_ref.dtype)
        o_ref[p, sub * rows + half:(sub + 1) * rows, :] = out_b.astype(o_ref.dtype)
        more = jnp.logical_or(jnp.logical_and(i >= 1, alive(n1[0], n1[1])),
                              jnp.logical_and(i >= 2, alive(m0[0], m0[1])))
        return flags | (more.astype(jnp.int32) << (p + pairs * sub))

    flags = jnp.int32(0)
    for sub in range(n_sub):
        for p in range(pairs):
            flags = fast_pair(p, sub, flags)

    def slow_pair(unit, _):
        @pl.when(((flags >> unit) & 1) == 1)
        def _():
            p = unit % pairs
            sub = unit // pairs
            first_kb = first_block(n_sub * step + sub)
            tile_rows = pl.ds(pl.multiple_of(sub * rows, rows), rows)
            q = q_ref[p, tile_rows, :] * SB_SCALE

            def block(kb, state, diagonal_rows):
                acc, m0, m1 = state
                ks = pl.ds(pl.multiple_of(kb * keys, keys), keys)
                kk = split_heads(k_ref[p, ks, :])
                vv = split_heads(v_ref[p, ks, :])

                def mask(x):
                    if diagonal_rows is None:
                        return x
                    lo = diagonal_rows
                    parts = [x[:lo]] if lo else []
                    parts.append(jnp.where(below_diagonal, x[lo:lo + half], 0.0))
                    if lo + half < rows:
                        parts.append(x[lo + half:])
                    return jnp.concatenate(parts, axis=0)

                z = lax.dot_general(q, kk, (((1,), (1,)), ((), ())),
                                    preferred_element_type=F32)
                sp = mask(softplus(z))
                later = jnp.dot(sp.astype(BF16), tri, preferred_element_type=F32)
                w = mask(jnp.exp(z - later - lane_mass(m0, m1)))
                out = jnp.dot(w.astype(BF16), vv, preferred_element_type=F32)
                t0, t1 = totals(later)
                return acc + out, m0 + t0, m1 + t1

            zero = jnp.zeros((half, 1), F32)
            big = jnp.concatenate([jnp.full((half, 1), 1e30, F32), zero], axis=0)
            acc, m0, m1 = block(first_kb + 1, (jnp.zeros((rows, LANES), F32), big, big), half)
            m0 = jnp.concatenate([zero, m0[half:]], axis=0)
            m1 = jnp.concatenate([zero, m1[half:]], axis=0)
            state = block(first_kb, (acc, m0, m1), 0)

            def cond(carry):
                kb, go, _ = carry
                return jnp.logical_and(kb >= ATTN_PAD // keys, go)

            def body(carry):
                kb, _, state = carry
                state = block(kb - 1, block(kb, state, None), None)
                return kb - 2, alive(state[1], state[2]), state

            _, _, state = lax.while_loop(cond, body, (first_kb - 1, True, state))
            o_ref[p, tile_rows, :] = state[0].astype(o_ref.dtype)
        return 0

    @pl.when(flags != 0)
    def _():
        lax.fori_loop(0, pairs * n_sub, slow_pair, 0)


def _attention(qkv):
    batch, n, padded, _ = qkv.shape
    seq = padded - ATTN_PAD
    pairs = n // N_QKV_TILES
    rows, keys = ATTN_STEP_ROWS, ATTN_KEYS
    upper = (jnp.arange(keys)[:, None] >= jnp.arange(keys)[None, :]).astype(BF16)
    zeros = jnp.zeros_like(upper)
    tri = jnp.block([[upper, zeros], [zeros, upper]])
    return pl.pallas_call(
        _attn_kernel,
        out_shape=jax.ShapeDtypeStruct((batch, pairs, seq, LANES), BF16),
        grid=(batch, seq // rows),
        in_specs=[pl.BlockSpec((None, pairs, rows, LANES),
                               lambda b, i: (b, 0, i + ATTN_PAD // ATTN_STEP_ROWS, 0)),
                  pl.BlockSpec((None, pairs, padded, LANES), lambda b, i: (b, 1, 0, 0)),
                  pl.BlockSpec((None, pairs, padded, LANES), lambda b, i: (b, 2, 0, 0)),
                  pl.BlockSpec((2 * keys, 2 * keys), lambda b, i: (0, 0))],
        out_specs=pl.BlockSpec((None, pairs, rows, LANES), lambda b, i: (b, 0, i, 0)),
        compiler_params=pltpu.CompilerParams(dimension_semantics=("arbitrary", "arbitrary"),
                                             vmem_limit_bytes=VMEM_LIMIT),
        name="sb_attention",
    )(qkv, qkv, qkv, tri)


def _silu_bf16(z):
    h = z * jnp.asarray(0.5, BF16)
    return h + h * jnp.tanh(h)


def _sigmoid_bf16(z):
    half = jnp.asarray(0.5, BF16)
    return half + half * jnp.tanh(z * half)


def _tail_kernel(attn_ref, za_ref, d_ref, zb_ref, ma_ref, mb_ref, x_ref, gate_ref,
                 poolw_ref, pscale_ref, wa_ref, wb_ref, wo_ref, fgain_ref,
                 o_ref, pooled_ref):
    pairs = attn_ref.shape[0]
    group = x_ref.shape[1] // len(POOL_WINDOWS)

    attn = jnp.concatenate([attn_ref[p] for p in range(pairs)], axis=1)
    a_in = attn * _silu_bf16(za_ref[...])
    p_a = jnp.dot(a_in, wa_ref[...], preferred_element_type=F32)

    for g in range(len(POOL_WINDOWS)):
        cols = slice(g * group, (g + 1) * group)
        mixed = jnp.dot(d_ref[:, cols], poolw_ref[g], preferred_element_type=F32)
        pooled_ref[:, cols] = (mixed * pscale_ref[:, cols]).astype(BF16)
    b_in = pooled_ref[...] * _silu_bf16(zb_ref[...])
    p_b = jnp.dot(b_in, wb_ref[...], preferred_element_type=F32)

    merged = (_sigmoid_bf16(ma_ref[...]) * p_a.astype(BF16)
              + _sigmoid_bf16(mb_ref[...]) * p_b.astype(BF16))
    out = jnp.dot(merged, wo_ref[...], preferred_element_type=F32)
    y = x_ref[...] + gate_ref[...] * out
    y = y * lax.rsqrt(jnp.mean(y * y, axis=-1, keepdims=True) + EPS)
    o_ref[...] = y * fgain_ref[...]


def _tail(attn, rest, x, gate, pool_w, pool_scale, w_branch_a, w_branch_b, w_out, final_gain):
    batch, seq, d = x.shape
    pairs = attn.shape[1]
    rows = TAIL_ROWS
    n_win = len(POOL_WINDOWS)
    group = d // n_win
    tile = lambda col: pl.BlockSpec((None, rows, d), lambda b, i: (b, i, col))
    full2 = lambda shape: pl.BlockSpec(shape, lambda b, i: (0, 0))
    full3 = lambda shape: pl.BlockSpec(shape, lambda b, i: (0, 0, 0))
    return pl.pallas_call(
        _tail_kernel,
        out_shape=jax.ShapeDtypeStruct((batch, seq, d), F32),
        grid=(batch, seq // rows),
        in_specs=[pl.BlockSpec((None, pairs, rows, LANES), lambda b, i: (b, 0, i, 0)),
                  tile(0),
                  tile(1),
                  tile(2), tile(3), tile(4),
                  tile(0),
                  pl.BlockSpec((None, 1, d), lambda b, i: (b, 0, 0)),
                  full3((n_win, group, group)),
                  full2((1, d)),
                  full2((d, d)), full2((d, d)), full2((d, d)),
                  full2((1, d))],
        out_specs=pl.BlockSpec((None, rows, d), lambda b, i: (b, i, 0)),
        scratch_shapes=[pltpu.VMEM((rows, d), BF16)],
        compiler_params=pltpu.CompilerParams(dimension_semantics=("arbitrary", "arbitrary"),
                                             vmem_limit_bytes=VMEM_LIMIT),
        name="tail",
    )(attn, rest, rest, rest, rest, rest, x, gate.reshape(batch, 1, d),
      pool_w.astype(BF16), pool_scale.reshape(1, d), w_branch_a.astype(BF16),
      w_branch_b.astype(BF16), w_out.astype(BF16), final_gain.reshape(1, d))


def kernel(x, c, norm_gain, w_ada, b_ada, w_in, pool_w, pool_scale, w_branch_a, w_branch_b,
           w_out, final_gain):
    depth = norm_gain.shape[0]
    assert depth == 1, "single-layer block"
    d = x.shape[-1]
    mod = _modulation(c, w_ada[0], b_ada[0])
    shift, scale, gate = mod[:, :d], mod[:, d:2 * d], mod[:, 2 * d:]
    qkv, rest = _projection(x, norm_gain[0], scale, shift, w_in[0].astype(BF16))
    attn = _attention(qkv)
    return _tail(attn, rest, x, gate, pool_w[0], pool_scale[0], w_branch_a[0], w_branch_b[0],
                 w_out[0], final_gain)
```

```python
import jax
import jax.numpy as jnp
from jax import lax
from jax.experimental import pallas as pl
from jax.experimental.pallas import tpu as pltpu

N_HEADS = 16
HEAD_DIM = 64
SB_SCALE = HEAD_DIM ** -0.5
POOL_WINDOWS = (2, 4, 8, 16)
MAX_WINDOW = max(POOL_WINDOWS)
EPS = 1e-6

LANES = 128
PROJ_COL_TILE = 1024
PROJ_ROW_CHUNK = 512
NORM_ROW_CHUNK = 256
ATTN_ROWS = 256
ATTN_STEP_ROWS = 512
ATTN_KEYS = 128
ATTN_PAD = 512
TAIL_ROWS = 512
VMEM_LIMIT = 48 * 1024 * 1024
DEAD_MASS = 104.0
EXP_CLAMP = 88.0

F32 = jnp.float32
BF16 = jnp.bfloat16


def _silu(z):
    return z * jax.nn.sigmoid(z)


def _mod_kernel(c_ref, w_ref, b_ref, o_ref):
    c = c_ref[...]
    o_ref[...] = jnp.dot(_silu(c), w_ref[...], preferred_element_type=F32,
                         precision=lax.Precision.HIGHEST) + b_ref[...]


def _modulation(c, w_ada, b_ada):
    batch, d = c.shape
    n = w_ada.shape[1]
    return pl.pallas_call(
        _mod_kernel,
        out_shape=jax.ShapeDtypeStruct((batch, n), F32),
        grid=(n // d,),
        in_specs=[pl.BlockSpec((batch, d), lambda j: (0, 0)),
                  pl.BlockSpec((d, d), lambda j: (0, j)),
                  pl.BlockSpec((1, d), lambda j: (0, j))],
        out_specs=pl.BlockSpec((batch, d), lambda j: (0, j)),
        compiler_params=pltpu.CompilerParams(dimension_semantics=("arbitrary",),
                                             vmem_limit_bytes=VMEM_LIMIT),
        name="adaln_mod",
    )(c, w_ada, b_ada.reshape(1, n))


N_QKV_TILES = 3
POOL_TILE = 4


def _proj_kernel(x_ref, gain_ref, scale_ref, shift_ref, w_ref, qkv_ref, rest_ref, h_ref):
    seq = x_ref.shape[0]
    j = pl.program_id(1)
    pairs = qkv_ref.shape[0]

    def norm_rows(rows):
        xf = x_ref[rows, :]
        y = xf * lax.rsqrt(jnp.mean(xf * xf, axis=-1, keepdims=True) + EPS)
        return ((y * gain_ref[...]) * (1.0 + scale_ref[...]) + shift_ref[...]).astype(BF16)

    def chunk(m):
        rows = slice(m * PROJ_ROW_CHUNK, (m + 1) * PROJ_ROW_CHUNK)
        return rows, jnp.dot(h_ref[rows, :], w_ref[...], preferred_element_type=F32)

    def store_qkv(rows, res):
        res = res.astype(BF16)
        for p in range(pairs):
            qkv_ref[p, ATTN_PAD + rows.start:ATTN_PAD + rows.stop, :] = (
                res[:, p * LANES:(p + 1) * LANES])

    zero_pad = jnp.zeros((pairs, ATTN_PAD, LANES), BF16)

    @pl.when(j == 0)
    def _():
        qkv_ref[:, 0:ATTN_PAD, :] = zero_pad
        for m in range(seq // PROJ_ROW_CHUNK):
            rows = slice(m * PROJ_ROW_CHUNK, (m + 1) * PROJ_ROW_CHUNK)
            for r in range(rows.start, rows.stop, NORM_ROW_CHUNK):
                sub = slice(r, r + NORM_ROW_CHUNK)
                h_ref[sub, :] = norm_rows(sub)
            store_qkv(*chunk(m))

    @pl.when(jnp.logical_and(j > 0, j < N_QKV_TILES))
    def _():
        qkv_ref[:, 0:ATTN_PAD, :] = zero_pad
        for m in range(seq // PROJ_ROW_CHUNK):
            store_qkv(*chunk(m))

    @pl.when(jnp.logical_and(j >= N_QKV_TILES, j != POOL_TILE))
    def _():
        for m in range(seq // PROJ_ROW_CHUNK):
            rows, res = chunk(m)
            rest_ref[rows, :] = res.astype(BF16)

    @pl.when(j == POOL_TILE)
    def _():
        group = w_ref.shape[1] // len(POOL_WINDOWS)

        def pool(rows, u, prev):
            pos = lax.broadcasted_iota(jnp.int32, (PROJ_ROW_CHUNK, 1), 0) + (rows.start + 1)
            for g, win in enumerate(POOL_WINDOWS):
                cols = slice(g * group, (g + 1) * group)
                s = jnp.concatenate([prev[:, cols], u[:, cols]], axis=0)
                shift = 1
                while shift < win:
                    s = s + pltpu.roll(s, shift, axis=0)
                    shift *= 2
                inv_count = 1.0 / jnp.minimum(pos, win).astype(F32)
                rest_ref[rows, cols] = (s[MAX_WINDOW:] * inv_count - u[:, cols]).astype(BF16)
            return u[PROJ_ROW_CHUNK - MAX_WINDOW:]

        prev = jnp.zeros((MAX_WINDOW, w_ref.shape[1]), F32)
        pending = chunk(0)
        for m in range(1, seq // PROJ_ROW_CHUNK):
            following = chunk(m)
            prev = pool(*pending, prev)
            pending = following
        pool(*pending, prev)


def _projection(x, norm_gain, scale, shift, w_in_bf16):
    batch, seq, d = x.shape
    n = w_in_bf16.shape[1]
    n_tiles = n // PROJ_COL_TILE
    pairs = PROJ_COL_TILE // LANES
    n_rest = n_tiles - N_QKV_TILES
    padded = seq + ATTN_PAD
    return pl.pallas_call(
        _proj_kernel,
        out_shape=(jax.ShapeDtypeStruct((batch, N_QKV_TILES * pairs, padded, LANES), BF16),
                   jax.ShapeDtypeStruct((batch, seq, n_rest * PROJ_COL_TILE), BF16)),
        grid=(batch, n_tiles),
        in_specs=[pl.BlockSpec((None, seq, d), lambda b, j: (b, 0, 0)),
                  pl.BlockSpec((1, d), lambda b, j: (0, 0)),
                  pl.BlockSpec((None, 1, d), lambda b, j: (b, 0, 0)),
                  pl.BlockSpec((None, 1, d), lambda b, j: (b, 0, 0)),
                  pl.BlockSpec((d, PROJ_COL_TILE), lambda b, j: (0, j))],
        out_specs=(pl.BlockSpec((None, pairs, padded, LANES),
                                lambda b, j: (b, jnp.minimum(j, N_QKV_TILES - 1), 0, 0)),
                   pl.BlockSpec((None, seq, PROJ_COL_TILE),
                                lambda b, j: (b, 0, jnp.maximum(j - N_QKV_TILES, 0)))),
        scratch_shapes=[pltpu.VMEM((seq, d), BF16)],
        compiler_params=pltpu.CompilerParams(dimension_semantics=("arbitrary", "arbitrary"),
                                             vmem_limit_bytes=VMEM_LIMIT),
        name="in_proj",
    )(x, norm_gain.reshape(1, d), scale.reshape(batch, 1, d), shift.reshape(batch, 1, d),
      w_in_bf16)


def _attn_kernel(q_ref, k_ref, v_ref, tri_ref, o_ref):
    pairs, step_rows, _ = q_ref.shape
    rows = ATTN_ROWS
    n_sub = step_rows // rows
    keys = ATTN_KEYS
    half = rows // 2
    assert half == keys and ATTN_PAD >= 2 * keys and ATTN_PAD % step_rows == 0
    tile = 2 * keys
    step = pl.program_id(1)
    first_head = lax.broadcasted_iota(jnp.int32, (1, LANES), 1) < HEAD_DIM
    row = lax.broadcasted_iota(jnp.int32, (half, tile), 0)
    key = lax.broadcasted_iota(jnp.int32, (half, tile), 1) & (keys - 1)
    below_diagonal = key < row
    tri = tri_ref[...]

    def first_block(i):
        return 2 * i + ATTN_PAD // keys

    def split_heads(x):
        zero = jnp.zeros((keys, LANES), x.dtype)
        parts = []
        for n in range(x.shape[0] // keys):
            blk = x[n * keys:(n + 1) * keys]
            parts += [jnp.where(first_head, blk, zero), jnp.where(first_head, zero, blk)]
        return jnp.concatenate(parts, axis=0)

    def softplus(z):
        return jnp.maximum(z, jnp.log(1.0 + jnp.exp(jnp.minimum(z, EXP_CLAMP))))

    def lane_mass(m0, m1):
        n = m0.shape[0]
        return jnp.concatenate([jnp.broadcast_to(m0, (n, keys)),
                                jnp.broadcast_to(m1, (n, keys))], axis=1)

    def totals(later):
        return later[:, 0:1], later[:, keys:keys + 1]

    def alive(m0, m1):
        return jnp.min(jnp.minimum(m0, m1)) < DEAD_MASS

    t = lambda x, n: x[:, n * tile:(n + 1) * tile]

    def fast_scores(p, sub):
        first_kb = first_block(n_sub * step + sub)
        q = q_ref[p, sub * rows:(sub + 1) * rows, :] * SB_SCALE
        win = pl.ds(pl.multiple_of((first_kb - 2) * keys, keys), 4 * keys)
        kk = split_heads(k_ref[p, win, :])
        vv = split_heads(v_ref[p, win, :])
        z = lax.dot_general(q, kk, (((1,), (1,)), ((), ())), preferred_element_type=F32)
        zt, zb = z[:half], z[half:]
        sp_t = [softplus(t(zt, 0)), softplus(t(zt, 1)),
                jnp.where(below_diagonal, softplus(t(zt, 2)), 0.0)]
        sp_b = [softplus(t(zb, 1)), softplus(t(zb, 2)),
                jnp.where(below_diagonal, softplus(t(zb, 3)), 0.0)]
        later_t = jnp.dot(jnp.concatenate(sp_t, axis=0).astype(BF16), tri,
                          preferred_element_type=F32)
        later_b = jnp.dot(jnp.concatenate(sp_b, axis=0).astype(BF16), tri,
                          preferred_element_type=F32)
        l_t = [later_t[n * half:(n + 1) * half] for n in range(3)]
        l_b = [later_b[n * half:(n + 1) * half] for n in range(3)]
        return zt, zb, l_t, l_b, vv

    def fast_weights(p, sub, flags, scores):
        zt, zb, l_t, l_b, vv = scores
        i = n_sub * step + sub
        m2 = totals(l_t[2])
        m1 = tuple(a + b for a, b in zip(m2, totals(l_t[1])))
        m0 = tuple(a + b for a, b in zip(m1, totals(l_t[0])))
        w_t = [jnp.exp(t(zt, 0) - l_t[0] - lane_mass(m1[0], m1[1])),
               jnp.exp(t(zt, 1) - l_t[1] - lane_mass(m2[0], m2[1])),
               jnp.where(below_diagonal, jnp.exp(t(zt, 2) - l_t[2]), 0.0)]
        n3 = totals(l_b[2])
        n2 = tuple(a + b for a, b in zip(n3, totals(l_b[1])))
        n1 = tuple(a + b for a, b in zip(n2, totals(l_b[0])))
        w_b = [jnp.exp(t(zb, 1) - l_b[0] - lane_mass(n2[0], n2[1])),
               jnp.exp(t(zb, 2) - l_b[1] - lane_mass(n3[0], n3[1])),
               jnp.where(below_diagonal, jnp.exp(t(zb, 3) - l_b[2]), 0.0)]
        out_t = jnp.dot(jnp.concatenate(w_t, axis=1).astype(BF16), vv[:3 * tile],
                        preferred_element_type=F32)
        out_b = jnp.dot(jnp.concatenate(w_b, axis=1).astype(BF16), vv[tile:],
                        preferred_element_type=F32)
        o_ref[p, sub * rows:sub * rows + half, :] = out_t.astype(o_ref.dtype)
        o_ref[p, sub * rows + half:(sub + 1) * rows, :] = out_b.astype(o_ref.dtype)
        more = jnp.logical_or(jnp.logical_and(i >= 1, alive(n1[0], n1[1])),
                              jnp.logical_and(i >= 2, alive(m0[0], m0[1])))
        return flags | (more.astype(jnp.int32) << (p + pairs * sub))

    units = [(p, sub) for sub in range(n_sub) for p in range(pairs)]
    staged = [fast_scores(p, sub) for p, sub in units]
    flags = jnp.int32(0)
    for (p, sub), scores in zip(units, staged):
        flags = fast_weights(p, sub, flags, scores)

    def slow_pair(unit, _):
        @pl.when(((flags >> unit) & 1) == 1)
        def _():
            p = unit % pairs
            sub = unit // pairs
            first_kb = first_block(n_sub * step + sub)
            tile_rows = pl.ds(pl.multiple_of(sub * rows, rows), rows)
            q = q_ref[p, tile_rows, :] * SB_SCALE

            def block(kb, state, diagonal_rows):
                acc, m0, m1 = state
                ks = pl.ds(pl.multiple_of(kb * keys, keys), keys)
                kk = split_heads(k_ref[p, ks, :])
                vv = split_heads(v_ref[p, ks, :])

                def mask(x):
                    if diagonal_rows is None:
                        return x
                    lo = diagonal_rows
                    parts = [x[:lo]] if lo else []
                    parts.append(jnp.where(below_diagonal, x[lo:lo + half], 0.0))
                    if lo + half < rows:
                        parts.append(x[lo + half:])
                    return jnp.concatenate(parts, axis=0)

                z = lax.dot_general(q, kk, (((1,), (1,)), ((), ())),
                                    preferred_element_type=F32)
                sp = mask(softplus(z))
                later = jnp.dot(sp.astype(BF16), tri, preferred_element_type=F32)
                w = mask(jnp.exp(z - later - lane_mass(m0, m1)))
                out = jnp.dot(w.astype(BF16), vv, preferred_element_type=F32)
                t0, t1 = totals(later)
                return acc + out, m0 + t0, m1 + t1

            zero = jnp.zeros((half, 1), F32)
            big = jnp.concatenate([jnp.full((half, 1), 1e30, F32), zero], axis=0)
            acc, m0, m1 = block(first_kb + 1, (jnp.zeros((rows, LANES), F32), big, big), half)
            m0 = jnp.concatenate([zero, m0[half:]], axis=0)
            m1 = jnp.concatenate([zero, m1[half:]], axis=0)
            state = block(first_kb, (acc, m0, m1), 0)

            def cond(carry):
                kb, go, _ = carry
                return jnp.logical_and(kb >= ATTN_PAD // keys, go)

            def body(carry):
                kb, _, state = carry
                state = block(kb - 1, block(kb, state, None), None)
                return kb - 2, alive(state[1], state[2]), state

            _, _, state = lax.while_loop(cond, body, (first_kb - 1, True, state))
            o_ref[p, tile_rows, :] = state[0].astype(o_ref.dtype)
        return 0

    @pl.when(flags != 0)
    def _():
        lax.fori_loop(0, pairs * n_sub, slow_pair, 0)


def _attention(qkv):
    batch, n, padded, _ = qkv.shape
    seq = padded - ATTN_PAD
    pairs = n // N_QKV_TILES
    rows, keys = ATTN_STEP_ROWS, ATTN_KEYS
    upper = (jnp.arange(keys)[:, None] >= jnp.arange(keys)[None, :]).astype(BF16)
    zeros = jnp.zeros_like(upper)
    tri = jnp.block([[upper, zeros], [zeros, upper]])
    return pl.pallas_call(
        _attn_kernel,
        out_shape=jax.ShapeDtypeStruct((batch, pairs, seq, LANES), BF16),
        grid=(batch, seq // rows),
        in_specs=[pl.BlockSpec((None, pairs, rows, LANES),
                               lambda b, i: (b, 0, i + ATTN_PAD // ATTN_STEP_ROWS, 0)),
                  pl.BlockSpec((None, pairs, padded, LANES), lambda b, i: (b, 1, 0, 0)),
                  pl.BlockSpec((None, pairs, padded, LANES), lambda b, i: (b, 2, 0, 0)),
                  pl.BlockSpec((2 * keys, 2 * keys), lambda b, i: (0, 0))],
        out_specs=pl.BlockSpec((None, pairs, rows, LANES), lambda b, i: (b, 0, i, 0)),
        compiler_params=pltpu.CompilerParams(dimension_semantics=("arbitrary", "arbitrary"),
                                             vmem_limit_bytes=VMEM_LIMIT),
        name="sb_attention",
    )(qkv, qkv, qkv, tri)


def _silu_bf16(z):
    h = z * jnp.asarray(0.5, BF16)
    return h + h * jnp.tanh(h)


def _sigmoid_bf16(z):
    half = jnp.asarray(0.5, BF16)
    return half + half * jnp.tanh(z * half)


def _tail_kernel(attn_ref, za_ref, d_ref, zb_ref, ma_ref, mb_ref, x_ref, gate_ref,
                 poolw_ref, pscale_ref, wa_ref, wb_ref, wo_ref, fgain_ref,
                 o_ref, pooled_ref):
    pairs = attn_ref.shape[0]
    group = x_ref.shape[1] // len(POOL_WINDOWS)

    attn = jnp.concatenate([attn_ref[p] for p in range(pairs)], axis=1)
    a_in = attn * _silu_bf16(za_ref[...])
    p_a = jnp.dot(a_in, wa_ref[...], preferred_element_type=F32)

    for g in range(len(POOL_WINDOWS)):
        cols = slice(g * group, (g + 1) * group)
        mixed = jnp.dot(d_ref[:, cols], poolw_ref[g], preferred_element_type=F32)
        pooled_ref[:, cols] = (mixed * pscale_ref[:, cols]).astype(BF16)
    b_in = pooled_ref[...] * _silu_bf16(zb_ref[...])
    p_b = jnp.dot(b_in, wb_ref[...], preferred_element_type=F32)

    merged = (_sigmoid_bf16(ma_ref[...]) * p_a.astype(BF16)
              + _sigmoid_bf16(mb_ref[...]) * p_b.astype(BF16))
    out = jnp.dot(merged, wo_ref[...], preferred_element_type=F32)
    y = x_ref[...] + gate_ref[...] * out
    y = y * lax.rsqrt(jnp.mean(y * y, axis=-1, keepdims=True) + EPS)
    o_ref[...] = y * fgain_ref[...]


def _tail(attn, rest, x, gate, pool_w, pool_scale, w_branch_a, w_branch_b, w_out, final_gain):
    batch, seq, d = x.shape
    pairs = attn.shape[1]
    rows = TAIL_ROWS
    n_win = len(POOL_WINDOWS)
    group = d // n_win
    tile = lambda col: pl.BlockSpec((None, rows, d), lambda b, i: (b, i, col))
    full2 = lambda shape: pl.BlockSpec(shape, lambda b, i: (0, 0))
    full3 = lambda shape: pl.BlockSpec(shape, lambda b, i: (0, 0, 0))
    return pl.pallas_call(
        _tail_kernel,
        out_shape=jax.ShapeDtypeStruct((batch, seq, d), F32),
        grid=(batch, seq // rows),
        in_specs=[pl.BlockSpec((None, pairs, rows, LANES), lambda b, i: (b, 0, i, 0)),
                  tile(0),
                  tile(1),
                  tile(2), tile(3), tile(4),
                  tile(0),
                  pl.BlockSpec((None, 1, d), lambda b, i: (b, 0, 0)),
                  full3((n_win, group, group)),
                  full2((1, d)),
                  full2((d, d)), full2((d, d)), full2((d, d)),
                  full2((1, d))],
        out_specs=pl.BlockSpec((None, rows, d), lambda b, i: (b, i, 0)),
        scratch_shapes=[pltpu.VMEM((rows, d), BF16)],
        compiler_params=pltpu.CompilerParams(dimension_semantics=("arbitrary", "arbitrary"),
                                             vmem_limit_bytes=VMEM_LIMIT),
        name="tail",
    )(attn, rest, rest, rest, rest, rest, x, gate.reshape(batch, 1, d),
      pool_w.astype(BF16), pool_scale.reshape(1, d), w_branch_a.astype(BF16),
      w_branch_b.astype(BF16), w_out.astype(BF16), final_gain.reshape(1, d))


def kernel(x, c, norm_gain, w_ada, b_ada, w_in, pool_w, pool_scale, w_branch_a, w_branch_b,
           w_out, final_gain):
    depth = norm_gain.shape[0]
    assert depth == 1, "single-layer block"
    d = x.shape[-1]
    mod = _modulation(c, w_ada[0], b_ada[0])
    shift, scale, gate = mod[:, :d], mod[:, d:2 * d], mod[:, 2 * d:]
    qkv, rest = _projection(x, norm_gain[0], scale, shift, w_in[0].astype(BF16))
    attn = _attention(qkv)
    return _tail(attn, rest, x, gate, pool_w[0], pool_scale[0], w_branch_a[0], w_branch_b[0],
                 w_out[0], final_gain)
```

```python
import jax
import jax.numpy as jnp
from jax import lax
from jax.experimental import pallas as pl
from jax.experimental.pallas import tpu as pltpu

N_HEADS = 16
HEAD_DIM = 64
SB_SCALE = HEAD_DIM ** -0.5
POOL_WINDOWS = (2, 4, 8, 16)
MAX_WINDOW = max(POOL_WINDOWS)
EPS = 1e-6

LANES = 128
PROJ_COL_TILE = 1024
PROJ_ROW_CHUNK = 512
NORM_ROW_CHUNK = 256
ATTN_ROWS = 256
ATTN_STEP_ROWS = 512
ATTN_KEYS = 128
ATTN_PAD = 512
SLOW_BLOCKS_PER_STEP = 4
TAIL_ROWS = 512
VMEM_LIMIT = 48 * 1024 * 1024
DEAD_MASS = 104.0
EXP_CLAMP = 88.0

F32 = jnp.float32
BF16 = jnp.bfloat16


def _silu(z):
    return z * jax.nn.sigmoid(z)


def _mod_kernel(c_ref, w_ref, b_ref, o_ref):
    c = c_ref[...]
    o_ref[...] = jnp.dot(_silu(c), w_ref[...], preferred_element_type=F32,
                         precision=lax.Precision.HIGHEST) + b_ref[...]


def _modulation(c, w_ada, b_ada):
    batch, d = c.shape
    n = w_ada.shape[1]
    return pl.pallas_call(
        _mod_kernel,
        out_shape=jax.ShapeDtypeStruct((batch, n), F32),
        grid=(n // d,),
        in_specs=[pl.BlockSpec((batch, d), lambda j: (0, 0)),
                  pl.BlockSpec((d, d), lambda j: (0, j)),
                  pl.BlockSpec((1, d), lambda j: (0, j))],
        out_specs=pl.BlockSpec((batch, d), lambda j: (0, j)),
        compiler_params=pltpu.CompilerParams(dimension_semantics=("arbitrary",),
                                             vmem_limit_bytes=VMEM_LIMIT),
        name="adaln_mod",
    )(c, w_ada, b_ada.reshape(1, n))


N_QKV_TILES = 3
POOL_TILE = 4


def _proj_kernel(x_ref, gain_ref, scale_ref, shift_ref, w_ref, qkv_ref, rest_ref, h_ref):
    seq = x_ref.shape[0]
    j = pl.program_id(1)
    pairs = qkv_ref.shape[0]

    def norm_rows(rows):
        xf = x_ref[rows, :]
        y = xf * lax.rsqrt(jnp.mean(xf * xf, axis=-1, keepdims=True) + EPS)
        return ((y * gain_ref[...]) * (1.0 + scale_ref[...]) + shift_ref[...]).astype(BF16)

    def chunk(m):
        rows = slice(m * PROJ_ROW_CHUNK, (m + 1) * PROJ_ROW_CHUNK)
        return rows, jnp.dot(h_ref[rows, :], w_ref[...], preferred_element_type=F32)

    def store_qkv(rows, res):
        res = res.astype(BF16)
        for p in range(pairs):
            qkv_ref[p, ATTN_PAD + rows.start:ATTN_PAD + rows.stop, :] = (
                res[:, p * LANES:(p + 1) * LANES])

    zero_pad = jnp.zeros((pairs, ATTN_PAD, LANES), BF16)

    @pl.when(j == 0)
    def _():
        qkv_ref[:, 0:ATTN_PAD, :] = zero_pad
        for m in range(seq // PROJ_ROW_CHUNK):
            rows = slice(m * PROJ_ROW_CHUNK, (m + 1) * PROJ_ROW_CHUNK)
            for r in range(rows.start, rows.stop, NORM_ROW_CHUNK):
                sub = slice(r, r + NORM_ROW_CHUNK)
                h_ref[sub, :] = norm_rows(sub)
            store_qkv(*chunk(m))

    @pl.when(jnp.logical_and(j > 0, j < N_QKV_TILES))
    def _():
        qkv_ref[:, 0:ATTN_PAD, :] = zero_pad
        for m in range(seq // PROJ_ROW_CHUNK):
            store_qkv(*chunk(m))

    @pl.when(jnp.logical_and(j >= N_QKV_TILES, j != POOL_TILE))
    def _():
        for m in range(seq // PROJ_ROW_CHUNK):
            rows, res = chunk(m)
            rest_ref[rows, :] = res.astype(BF16)

    @pl.when(j == POOL_TILE)
    def _():
        group = w_ref.shape[1] // len(POOL_WINDOWS)

        def pool(rows, u, prev):
            pos = lax.broadcasted_iota(jnp.int32, (PROJ_ROW_CHUNK, 1), 0) + (rows.start + 1)
            for g, win in enumerate(POOL_WINDOWS):
                cols = slice(g * group, (g + 1) * group)
                s = jnp.concatenate([prev[:, cols], u[:, cols]], axis=0)
                shift = 1
                while shift < win:
                    s = s + pltpu.roll(s, shift, axis=0)
                    shift *= 2
                inv_count = 1.0 / jnp.minimum(pos, win).astype(F32)
                rest_ref[rows, cols] = (s[MAX_WINDOW:] * inv_count - u[:, cols]).astype(BF16)
            return u[PROJ_ROW_CHUNK - MAX_WINDOW:]

        prev = jnp.zeros((MAX_WINDOW, w_ref.shape[1]), F32)
        for rows, u in [chunk(m) for m in range(seq // PROJ_ROW_CHUNK)]:
            prev = pool(rows, u, prev)


def _projection(x, norm_gain, scale, shift, w_in_bf16):
    batch, seq, d = x.shape
    n = w_in_bf16.shape[1]
    n_tiles = n // PROJ_COL_TILE
    pairs = PROJ_COL_TILE // LANES
    n_rest = n_tiles - N_QKV_TILES
    padded = seq + ATTN_PAD
    return pl.pallas_call(
        _proj_kernel,
        out_shape=(jax.ShapeDtypeStruct((batch, N_QKV_TILES * pairs, padded, LANES), BF16),
                   jax.ShapeDtypeStruct((batch, seq, n_rest * PROJ_COL_TILE), BF16)),
        grid=(batch, n_tiles),
        in_specs=[pl.BlockSpec((None, seq, d), lambda b, j: (b, 0, 0)),
                  pl.BlockSpec((1, d), lambda b, j: (0, 0)),
                  pl.BlockSpec((None, 1, d), lambda b, j: (b, 0, 0)),
                  pl.BlockSpec((None, 1, d), lambda b, j: (b, 0, 0)),
                  pl.BlockSpec((d, PROJ_COL_TILE), lambda b, j: (0, j))],
        out_specs=(pl.BlockSpec((None, pairs, padded, LANES),
                                lambda b, j: (b, jnp.minimum(j, N_QKV_TILES - 1), 0, 0)),
                   pl.BlockSpec((None, seq, PROJ_COL_TILE),
                                lambda b, j: (b, 0, jnp.maximum(j - N_QKV_TILES, 0)))),
        scratch_shapes=[pltpu.VMEM((seq, d), BF16)],
        compiler_params=pltpu.CompilerParams(dimension_semantics=("arbitrary", "arbitrary"),
                                             vmem_limit_bytes=VMEM_LIMIT),
        name="in_proj",
    )(x, norm_gain.reshape(1, d), scale.reshape(batch, 1, d), shift.reshape(batch, 1, d),
      w_in_bf16)


def _attn_kernel(q_ref, k_ref, v_ref, tri_ref, o_ref):
    pairs, step_rows, _ = q_ref.shape
    rows = ATTN_ROWS
    n_sub = step_rows // rows
    keys = ATTN_KEYS
    half = rows // 2
    assert half == keys and ATTN_PAD >= 2 * keys and ATTN_PAD % step_rows == 0
    assert SLOW_BLOCKS_PER_STEP * keys <= ATTN_PAD
    tile = 2 * keys
    step = pl.program_id(1)
    first_head = lax.broadcasted_iota(jnp.int32, (1, LANES), 1) < HEAD_DIM
    row = lax.broadcasted_iota(jnp.int32, (half, tile), 0)
    key = lax.broadcasted_iota(jnp.int32, (half, tile), 1) & (keys - 1)
    below_diagonal = key < row
    tri = tri_ref[...]

    def first_block(i):
        return 2 * i + ATTN_PAD // keys

    def split_heads(x):
        zero = jnp.zeros((keys, LANES), x.dtype)
        parts = []
        for n in range(x.shape[0] // keys):
            blk = x[n * keys:(n + 1) * keys]
            parts += [jnp.where(first_head, blk, zero), jnp.where(first_head, zero, blk)]
        return jnp.concatenate(parts, axis=0)

    def softplus(z):
        return jnp.maximum(z, jnp.log(1.0 + jnp.exp(jnp.minimum(z, EXP_CLAMP))))

    def lane_mass(m0, m1):
        n = m0.shape[0]
        return jnp.concatenate([jnp.broadcast_to(m0, (n, keys)),
                                jnp.broadcast_to(m1, (n, keys))], axis=1)

    def totals(later):
        return later[:, 0:1], later[:, keys:keys + 1]

    def alive(m0, m1):
        return jnp.min(jnp.minimum(m0, m1)) < DEAD_MASS

    t = lambda x, n: x[:, n * tile:(n + 1) * tile]

    def fast_scores(p, sub):
        first_kb = first_block(n_sub * step + sub)
        q = q_ref[p, sub * rows:(sub + 1) * rows, :] * SB_SCALE
        win = pl.ds(pl.multiple_of((first_kb - 2) * keys, keys), 4 * keys)
        kk = split_heads(k_ref[p, win, :])
        vv = split_heads(v_ref[p, win, :])
        z = lax.dot_general(q, kk, (((1,), (1,)), ((), ())), preferred_element_type=F32)
        zt, zb = z[:half], z[half:]
        sp_t = [softplus(t(zt, 0)), softplus(t(zt, 1)),
                jnp.where(below_diagonal, softplus(t(zt, 2)), 0.0)]
        sp_b = [softplus(t(zb, 1)), softplus(t(zb, 2)),
                jnp.where(below_diagonal, softplus(t(zb, 3)), 0.0)]
        later_t = jnp.dot(jnp.concatenate(sp_t, axis=0).astype(BF16), tri,
                          preferred_element_type=F32)
        later_b = jnp.dot(jnp.concatenate(sp_b, axis=0).astype(BF16), tri,
                          preferred_element_type=F32)
        l_t = [later_t[n * half:(n + 1) * half] for n in range(3)]
        l_b = [later_b[n * half:(n + 1) * half] for n in range(3)]
        return zt, zb, l_t, l_b, vv

    def fast_weights(p, sub, flags, scores):
        zt, zb, l_t, l_b, vv = scores
        i = n_sub * step + sub
        m2 = totals(l_t[2])
        m1 = tuple(a + b for a, b in zip(m2, totals(l_t[1])))
        m0 = tuple(a + b for a, b in zip(m1, totals(l_t[0])))
        w_t = [jnp.exp(t(zt, 0) - l_t[0] - lane_mass(m1[0], m1[1])),
               jnp.exp(t(zt, 1) - l_t[1] - lane_mass(m2[0], m2[1])),
               jnp.where(below_diagonal, jnp.exp(t(zt, 2) - l_t[2]), 0.0)]
        n3 = totals(l_b[2])
        n2 = tuple(a + b for a, b in zip(n3, totals(l_b[1])))
        n1 = tuple(a + b for a, b in zip(n2, totals(l_b[0])))
        w_b = [jnp.exp(t(zb, 1) - l_b[0] - lane_mass(n2[0], n2[1])),
               jnp.exp(t(zb, 2) - l_b[1] - lane_mass(n3[0], n3[1])),
               jnp.where(below_diagonal, jnp.exp(t(zb, 3) - l_b[2]), 0.0)]
        out_t = jnp.dot(jnp.concatenate(w_t, axis=1).astype(BF16), vv[:3 * tile],
                        preferred_element_type=F32)
        out_b = jnp.dot(jnp.concatenate(w_b, axis=1).astype(BF16), vv[tile:],
                        preferred_element_type=F32)
        o_ref[p, sub * rows:sub * rows + half, :] = out_t.astype(o_ref.dtype)
        o_ref[p, sub * rows + half:(sub + 1) * rows, :] = out_b.astype(o_ref.dtype)
        more = jnp.logical_or(jnp.logical_and(i >= 1, alive(n1[0], n1[1])),
                              jnp.logical_and(i >= 2, alive(m0[0], m0[1])))
        return flags | (more.astype(jnp.int32) << (p + pairs * sub))

    units = [(p, sub) for sub in range(n_sub) for p in range(pairs)]
    staged = [fast_scores(p, sub) for p, sub in units]
    flags = jnp.int32(0)
    for (p, sub), scores in zip(units, staged):
        flags = fast_weights(p, sub, flags, scores)

    def slow_pair(unit, _):
        @pl.when(((flags >> unit) & 1) == 1)
        def _():
            p = unit % pairs
            sub = unit // pairs
            first_kb = first_block(n_sub * step + sub)
            tile_rows = pl.ds(pl.multiple_of(sub * rows, rows), rows)
            q = q_ref[p, tile_rows, :] * SB_SCALE

            def mask(x, diagonal_rows):
                if diagonal_rows is None:
                    return x
                lo = diagonal_rows
                parts = [x[:lo]] if lo else []
                parts.append(jnp.where(below_diagonal, x[lo:lo + half], 0.0))
                if lo + half < rows:
                    parts.append(x[lo + half:])
                return jnp.concatenate(parts, axis=0)

            def scores(kb, diagonal_rows):
                ks = pl.ds(pl.multiple_of(kb * keys, keys), keys)
                kk = split_heads(k_ref[p, ks, :])
                z = lax.dot_general(q, kk, (((1,), (1,)), ((), ())),
                                    preferred_element_type=F32)
                sp = mask(softplus(z), diagonal_rows)
                later = jnp.dot(sp.astype(BF16), tri, preferred_element_type=F32)
                return ks, z, later, diagonal_rows

            def weights(state, block_scores):
                acc, m0, m1 = state
                ks, z, later, diagonal_rows = block_scores
                vv = split_heads(v_ref[p, ks, :])
                w = mask(jnp.exp(z - later - lane_mass(m0, m1)), diagonal_rows)
                out = jnp.dot(w.astype(BF16), vv, preferred_element_type=F32)
                t0, t1 = totals(later)
                return acc + out, m0 + t0, m1 + t1

            zero = jnp.zeros((half, 1), F32)
            big = jnp.concatenate([jnp.full((half, 1), 1e30, F32), zero], axis=0)
            lower, upper = scores(first_kb + 1, half), scores(first_kb, 0)
            acc, m0, m1 = weights((jnp.zeros((rows, LANES), F32), big, big), lower)
            m0 = jnp.concatenate([zero, m0[half:]], axis=0)
            m1 = jnp.concatenate([zero, m1[half:]], axis=0)
            state = weights((acc, m0, m1), upper)

            def cond(carry):
                kb, go, _ = carry
                return jnp.logical_and(kb >= ATTN_PAD // keys, go)

            def body(carry):
                kb, _, state = carry
                staged = [scores(kb - n, None) for n in range(SLOW_BLOCKS_PER_STEP)]
                for block_scores in staged:
                    state = weights(state, block_scores)
                return kb - SLOW_BLOCKS_PER_STEP, alive(state[1], state[2]), state

            _, _, state = lax.while_loop(cond, body, (first_kb - 1, True, state))
            o_ref[p, tile_rows, :] = state[0].astype(o_ref.dtype)
        return 0

    @pl.when(flags != 0)
    def _():
        lax.fori_loop(0, pairs * n_sub, slow_pair, 0)


def _attention(qkv):
    batch, n, padded, _ = qkv.shape
    seq = padded - ATTN_PAD
    pairs = n // N_QKV_TILES
    rows, keys = ATTN_STEP_ROWS, ATTN_KEYS
    upper = (jnp.arange(keys)[:, None] >= jnp.arange(keys)[None, :]).astype(BF16)
    zeros = jnp.zeros_like(upper)
    tri = jnp.block([[upper, zeros], [zeros, upper]])
    return pl.pallas_call(
        _attn_kernel,
        out_shape=jax.ShapeDtypeStruct((batch, pairs, seq, LANES), BF16),
        grid=(batch, seq // rows),
        in_specs=[pl.BlockSpec((None, pairs, rows, LANES),
                               lambda b, i: (b, 0, i + ATTN_PAD // ATTN_STEP_ROWS, 0)),
                  pl.BlockSpec((None, pairs, padded, LANES), lambda b, i: (b, 1, 0, 0)),
                  pl.BlockSpec((None, pairs, padded, LANES), lambda b, i: (b, 2, 0, 0)),
                  pl.BlockSpec((2 * keys, 2 * keys), lambda b, i: (0, 0))],
        out_specs=pl.BlockSpec((None, pairs, rows, LANES), lambda b, i: (b, 0, i, 0)),
        compiler_params=pltpu.CompilerParams(dimension_semantics=("arbitrary", "arbitrary"),
                                             vmem_limit_bytes=VMEM_LIMIT),
        name="sb_attention",
    )(qkv, qkv, qkv, tri)


def _silu_bf16(z):
    h = z * jnp.asarray(0.5, BF16)
    return h + h * jnp.tanh(h)


def _sigmoid_bf16(z):
    half = jnp.asarray(0.5, BF16)
    return half + half * jnp.tanh(z * half)


def _tail_kernel(attn_ref, za_ref, d_ref, zb_ref, ma_ref, mb_ref, x_ref, gate_ref,
                 poolw_ref, pscale_ref, wa_ref, wb_ref, wo_ref, fgain_ref,
                 o_ref, pooled_ref):
    pairs = attn_ref.shape[0]
    group = x_ref.shape[1] // len(POOL_WINDOWS)

    attn = jnp.concatenate([attn_ref[p] for p in range(pairs)], axis=1)
    a_in = attn * _silu_bf16(za_ref[...])
    p_a = jnp.dot(a_in, wa_ref[...], preferred_element_type=F32)

    for g in range(len(POOL_WINDOWS)):
        cols = slice(g * group, (g + 1) * group)
        mixed = jnp.dot(d_ref[:, cols], poolw_ref[g], preferred_element_type=F32)
        pooled_ref[:, cols] = (mixed * pscale_ref[:, cols]).astype(BF16)
    b_in = pooled_ref[...] * _silu_bf16(zb_ref[...])
    p_b = jnp.dot(b_in, wb_ref[...], preferred_element_type=F32)

    merged = (_sigmoid_bf16(ma_ref[...]) * p_a.astype(BF16)
              + _sigmoid_bf16(mb_ref[...]) * p_b.astype(BF16))
    out = jnp.dot(merged, wo_ref[...], preferred_element_type=F32)
    y = x_ref[...] + gate_ref[...] * out
    y = y * lax.rsqrt(jnp.mean(y * y, axis=-1, keepdims=True) + EPS)
    o_ref[...] = y * fgain_ref[...]


def _tail(attn, rest, x, gate, pool_w, pool_scale, w_branch_a, w_branch_b, w_out, final_gain):
    batch, seq, d = x.shape
    pairs = attn.shape[1]
    rows = TAIL_ROWS
    n_win = len(POOL_WINDOWS)
    group = d // n_win
    tile = lambda col: pl.BlockSpec((None, rows, d), lambda b, i: (b, i, col))
    full2 = lambda shape: pl.BlockSpec(shape, lambda b, i: (0, 0))
    full3 = lambda shape: pl.BlockSpec(shape, lambda b, i: (0, 0, 0))
    return pl.pallas_call(
        _tail_kernel,
        out_shape=jax.ShapeDtypeStruct((batch, seq, d), F32),
        grid=(batch, seq // rows),
        in_specs=[pl.BlockSpec((None, pairs, rows, LANES), lambda b, i: (b, 0, i, 0)),
                  tile(0),
                  tile(1),
                  tile(2), tile(3), tile(4),
                  tile(0),
                  pl.BlockSpec((None, 1, d), lambda b, i: (b, 0, 0)),
                  full3((n_win, group, group)),
                  full2((1, d)),
                  full2((d, d)), full2((d, d)), full2((d, d)),
                  full2((1, d))],
        out_specs=pl.BlockSpec((None, rows, d), lambda b, i: (b, i, 0)),
        scratch_shapes=[pltpu.VMEM((rows, d), BF16)],
        compiler_params=pltpu.CompilerParams(dimension_semantics=("arbitrary", "arbitrary"),
                                             vmem_limit_bytes=VMEM_LIMIT),
        name="tail",
    )(attn, rest, rest, rest, rest, rest, x, gate.reshape(batch, 1, d),
      pool_w.astype(BF16), pool_scale.reshape(1, d), w_branch_a.astype(BF16),
      w_branch_b.astype(BF16), w_out.astype(BF16), final_gain.reshape(1, d))


def kernel(x, c, norm_gain, w_ada, b_ada, w_in, pool_w, pool_scale, w_branch_a, w_branch_b,
           w_out, final_gain):
    depth = norm_gain.shape[0]
    assert depth == 1, "single-layer block"
    d = x.shape[-1]
    mod = _modulation(c, w_ada[0], b_ada[0])
    shift, scale, gate = mod[:, :d], mod[:, d:2 * d], mod[:, 2 * d:]
    qkv, rest = _projection(x, norm_gain[0], scale, shift, w_in[0].astype(BF16))
    attn = _attention(qkv)
    return _tail(attn, rest, x, gate, pool_w[0], pool_scale[0], w_branch_a[0], w_branch_b[0],
                 w_out[0], final_gain)
```

```python
import jax
import jax.numpy as jnp
from jax import lax
from jax.experimental import pallas as pl
from jax.experimental.pallas import tpu as pltpu

N_HEADS = 16
HEAD_DIM = 64
SB_SCALE = HEAD_DIM ** -0.5
POOL_WINDOWS = (2, 4, 8, 16)
MAX_WINDOW = max(POOL_WINDOWS)
EPS = 1e-6

LANES = 128
PROJ_COL_TILE = 1024
PROJ_ROW_CHUNK = 512
NORM_ROW_CHUNK = 256
ATTN_ROWS = 256
ATTN_STEP_ROWS = 512
ATTN_KEYS = 128
ATTN_PAD = 512
SLOW_BLOCKS_PER_STEP = 4
TAIL_ROWS = 512
VMEM_LIMIT = 48 * 1024 * 1024
DEAD_MASS = 151.0
EXP2_CLAMP = 126.0
Q_SCALE = SB_SCALE * 1.4426950408889634

F32 = jnp.float32
BF16 = jnp.bfloat16


def _silu(z):
    return z * jax.nn.sigmoid(z)


def _mod_kernel(c_ref, w_ref, b_ref, o_ref):
    c = c_ref[...]
    o_ref[...] = jnp.dot(_silu(c), w_ref[...], preferred_element_type=F32,
                         precision=lax.Precision.HIGHEST) + b_ref[...]


def _modulation(c, w_ada, b_ada):
    batch, d = c.shape
    n = w_ada.shape[1]
    return pl.pallas_call(
        _mod_kernel,
        out_shape=jax.ShapeDtypeStruct((batch, n), F32),
        grid=(n // d,),
        in_specs=[pl.BlockSpec((batch, d), lambda j: (0, 0)),
                  pl.BlockSpec((d, d), lambda j: (0, j)),
                  pl.BlockSpec((1, d), lambda j: (0, j))],
        out_specs=pl.BlockSpec((batch, d), lambda j: (0, j)),
        compiler_params=pltpu.CompilerParams(dimension_semantics=("arbitrary",),
                                             vmem_limit_bytes=VMEM_LIMIT),
        name="adaln_mod",
    )(c, w_ada, b_ada.reshape(1, n))


N_QKV_TILES = 3
POOL_TILE = 4


def _proj_kernel(x_ref, gain_ref, scale_ref, shift_ref, w_ref, qkv_ref, rest_ref, h_ref):
    seq = x_ref.shape[0]
    j = pl.program_id(1)
    pairs = qkv_ref.shape[0]

    def norm_rows(rows):
        xf = x_ref[rows, :]
        y = xf * lax.rsqrt(jnp.mean(xf * xf, axis=-1, keepdims=True) + EPS)
        return ((y * gain_ref[...]) * (1.0 + scale_ref[...]) + shift_ref[...]).astype(BF16)

    def chunk(m):
        rows = slice(m * PROJ_ROW_CHUNK, (m + 1) * PROJ_ROW_CHUNK)
        return rows, jnp.dot(h_ref[rows, :], w_ref[...], preferred_element_type=F32)

    def store_qkv(rows, res):
        res = (res * jnp.where(j == 0, Q_SCALE, 1.0).astype(F32)).astype(BF16)
        for p in range(pairs):
            qkv_ref[p, ATTN_PAD + rows.start:ATTN_PAD + rows.stop, :] = (
                res[:, p * LANES:(p + 1) * LANES])

    zero_pad = jnp.zeros((pairs, ATTN_PAD, LANES), BF16)

    @pl.when(j == 0)
    def _():
        qkv_ref[:, 0:ATTN_PAD, :] = zero_pad
        for m in range(seq // PROJ_ROW_CHUNK):
            rows = slice(m * PROJ_ROW_CHUNK, (m + 1) * PROJ_ROW_CHUNK)
            for r in range(rows.start, rows.stop, NORM_ROW_CHUNK):
                sub = slice(r, r + NORM_ROW_CHUNK)
                h_ref[sub, :] = norm_rows(sub)
            store_qkv(*chunk(m))

    @pl.when(jnp.logical_and(j > 0, j < N_QKV_TILES))
    def _():
        qkv_ref[:, 0:ATTN_PAD, :] = zero_pad
        for m in range(seq // PROJ_ROW_CHUNK):
            store_qkv(*chunk(m))

    @pl.when(jnp.logical_and(j >= N_QKV_TILES, j != POOL_TILE))
    def _():
        for m in range(seq // PROJ_ROW_CHUNK):
            rows, res = chunk(m)
            rest_ref[rows, :] = res.astype(BF16)

    @pl.when(j == POOL_TILE)
    def _():
        group = w_ref.shape[1] // len(POOL_WINDOWS)

        def pool(rows, u, prev):
            pos = lax.broadcasted_iota(jnp.int32, (PROJ_ROW_CHUNK, 1), 0) + (rows.start + 1)
            for g, win in enumerate(POOL_WINDOWS):
                cols = slice(g * group, (g + 1) * group)
                s = jnp.concatenate([prev[:, cols], u[:, cols]], axis=0)
                shift = 1
                while shift < win:
                    s = s + pltpu.roll(s, shift, axis=0)
                    shift *= 2
                inv_count = 1.0 / jnp.minimum(pos, win).astype(F32)
                rest_ref[rows, cols] = (s[MAX_WINDOW:] * inv_count - u[:, cols]).astype(BF16)
            return u[PROJ_ROW_CHUNK - MAX_WINDOW:]

        prev = jnp.zeros((MAX_WINDOW, w_ref.shape[1]), F32)
        for rows, u in [chunk(m) for m in range(seq // PROJ_ROW_CHUNK)]:
            prev = pool(rows, u, prev)


def _projection(x, norm_gain, scale, shift, w_in_bf16):
    batch, seq, d = x.shape
    n = w_in_bf16.shape[1]
    n_tiles = n // PROJ_COL_TILE
    pairs = PROJ_COL_TILE // LANES
    n_rest = n_tiles - N_QKV_TILES
    padded = seq + ATTN_PAD
    return pl.pallas_call(
        _proj_kernel,
        out_shape=(jax.ShapeDtypeStruct((batch, N_QKV_TILES * pairs, padded, LANES), BF16),
                   jax.ShapeDtypeStruct((batch, seq, n_rest * PROJ_COL_TILE), BF16)),
        grid=(batch, n_tiles),
        in_specs=[pl.BlockSpec((None, seq, d), lambda b, j: (b, 0, 0)),
                  pl.BlockSpec((1, d), lambda b, j: (0, 0)),
                  pl.BlockSpec((None, 1, d), lambda b, j: (b, 0, 0)),
                  pl.BlockSpec((None, 1, d), lambda b, j: (b, 0, 0)),
                  pl.BlockSpec((d, PROJ_COL_TILE), lambda b, j: (0, j))],
        out_specs=(pl.BlockSpec((None, pairs, padded, LANES),
                                lambda b, j: (b, jnp.minimum(j, N_QKV_TILES - 1), 0, 0)),
                   pl.BlockSpec((None, seq, PROJ_COL_TILE),
                                lambda b, j: (b, 0, jnp.maximum(j - N_QKV_TILES, 0)))),
        scratch_shapes=[pltpu.VMEM((seq, d), BF16)],
        compiler_params=pltpu.CompilerParams(dimension_semantics=("arbitrary", "arbitrary"),
                                             vmem_limit_bytes=VMEM_LIMIT),
        name="in_proj",
    )(x, norm_gain.reshape(1, d), scale.reshape(batch, 1, d), shift.reshape(batch, 1, d),
      w_in_bf16)


def _attn_kernel(q_ref, k_ref, v_ref, tri_ref, o_ref):
    pairs, step_rows, _ = q_ref.shape
    rows = ATTN_ROWS
    n_sub = step_rows // rows
    keys = ATTN_KEYS
    half = rows // 2
    assert half == keys and ATTN_PAD >= 2 * keys and ATTN_PAD % step_rows == 0
    assert SLOW_BLOCKS_PER_STEP * keys <= ATTN_PAD
    tile = 2 * keys
    step = pl.program_id(1)
    first_head = lax.broadcasted_iota(jnp.int32, (1, LANES), 1) < HEAD_DIM
    row = lax.broadcasted_iota(jnp.int32, (half, tile), 0)
    key = lax.broadcasted_iota(jnp.int32, (half, tile), 1) & (keys - 1)
    below_diagonal = key < row
    tri = tri_ref[...]

    def first_block(i):
        return 2 * i + ATTN_PAD // keys

    def split_heads(x):
        zero = jnp.zeros((keys, LANES), x.dtype)
        parts = []
        for n in range(x.shape[0] // keys):
            blk = x[n * keys:(n + 1) * keys]
            parts += [jnp.where(first_head, blk, zero), jnp.where(first_head, zero, blk)]
        return jnp.concatenate(parts, axis=0)

    def softplus(z):
        return jnp.maximum(z, jnp.log2(1.0 + jnp.exp2(jnp.minimum(z, EXP2_CLAMP))))

    def lane_mass(m0, m1):
        n = m0.shape[0]
        return jnp.concatenate([jnp.broadcast_to(m0, (n, keys)),
                                jnp.broadcast_to(m1, (n, keys))], axis=1)

    def totals(later):
        return later[:, 0:1], later[:, keys:keys + 1]

    def alive(m0, m1):
        return jnp.min(jnp.minimum(m0, m1)) < DEAD_MASS

    t = lambda x, n: x[:, n * tile:(n + 1) * tile]

    def fast_scores(p, sub):
        first_kb = first_block(n_sub * step + sub)
        q = q_ref[p, sub * rows:(sub + 1) * rows, :]
        win = pl.ds(pl.multiple_of((first_kb - 2) * keys, keys), 4 * keys)
        kk = split_heads(k_ref[p, win, :])
        vv = split_heads(v_ref[p, win, :])
        z = lax.dot_general(q, kk, (((1,), (1,)), ((), ())), preferred_element_type=F32)
        zt, zb = z[:half], z[half:]
        sp_t = [softplus(t(zt, 0)), softplus(t(zt, 1)),
                jnp.where(below_diagonal, softplus(t(zt, 2)), 0.0)]
        sp_b = [softplus(t(zb, 1)), softplus(t(zb, 2)),
                jnp.where(below_diagonal, softplus(t(zb, 3)), 0.0)]
        later_t = jnp.dot(jnp.concatenate(sp_t, axis=0).astype(BF16), tri,
                          preferred_element_type=F32)
        later_b = jnp.dot(jnp.concatenate(sp_b, axis=0).astype(BF16), tri,
                          preferred_element_type=F32)
        l_t = [later_t[n * half:(n + 1) * half] for n in range(3)]
        l_b = [later_b[n * half:(n + 1) * half] for n in range(3)]
        return zt, zb, l_t, l_b, vv

    def fast_weights(p, sub, flags, scores):
        zt, zb, l_t, l_b, vv = scores
        i = n_sub * step + sub
        m2 = totals(l_t[2])
        m1 = tuple(a + b for a, b in zip(m2, totals(l_t[1])))
        m0 = tuple(a + b for a, b in zip(m1, totals(l_t[0])))
        w_t = [jnp.exp2(t(zt, 0) - l_t[0] - lane_mass(m1[0], m1[1])),
               jnp.exp2(t(zt, 1) - l_t[1] - lane_mass(m2[0], m2[1])),
               jnp.where(below_diagonal, jnp.exp2(t(zt, 2) - l_t[2]), 0.0)]
        n3 = totals(l_b[2])
        n2 = tuple(a + b for a, b in zip(n3, totals(l_b[1])))
        n1 = tuple(a + b for a, b in zip(n2, totals(l_b[0])))
        w_b = [jnp.exp2(t(zb, 1) - l_b[0] - lane_mass(n2[0], n2[1])),
               jnp.exp2(t(zb, 2) - l_b[1] - lane_mass(n3[0], n3[1])),
               jnp.where(below_diagonal, jnp.exp2(t(zb, 3) - l_b[2]), 0.0)]
        out_t = jnp.dot(jnp.concatenate(w_t, axis=1).astype(BF16), vv[:3 * tile],
                        preferred_element_type=F32)
        out_b = jnp.dot(jnp.concatenate(w_b, axis=1).astype(BF16), vv[tile:],
                        preferred_element_type=F32)
        o_ref[p, sub * rows:sub * rows + half, :] = out_t.astype(o_ref.dtype)
        o_ref[p, sub * rows + half:(sub + 1) * rows, :] = out_b.astype(o_ref.dtype)
        more = jnp.logical_or(jnp.logical_and(i >= 1, alive(n1[0], n1[1])),
                              jnp.logical_and(i >= 2, alive(m0[0], m0[1])))
        return flags | (more.astype(jnp.int32) << (p + pairs * sub))

    units = [(p, sub) for sub in range(n_sub) for p in range(pairs)]
    staged = [fast_scores(p, sub) for p, sub in units]
    flags = jnp.int32(0)
    for (p, sub), scores in zip(units, staged):
        flags = fast_weights(p, sub, flags, scores)

    def slow_pair(unit, _):
        @pl.when(((flags >> unit) & 1) == 1)
        def _():
            p = unit % pairs
            sub = unit // pairs
            first_kb = first_block(n_sub * step + sub)
            tile_rows = pl.ds(pl.multiple_of(sub * rows, rows), rows)
            q = q_ref[p, tile_rows, :]

            def mask(x, diagonal_rows):
                if diagonal_rows is None:
                    return x
                lo = diagonal_rows
                parts = [x[:lo]] if lo else []
                parts.append(jnp.where(below_diagonal, x[lo:lo + half], 0.0))
                if lo + half < rows:
                    parts.append(x[lo + half:])
                return jnp.concatenate(parts, axis=0)

            def scores(kb, diagonal_rows):
                ks = pl.ds(pl.multiple_of(kb * keys, keys), keys)
                kk = split_heads(k_ref[p, ks, :])
                z = lax.dot_general(q, kk, (((1,), (1,)), ((), ())),
                                    preferred_element_type=F32)
                sp = mask(softplus(z), diagonal_rows)
                later = jnp.dot(sp.astype(BF16), tri, preferred_element_type=F32)
                return ks, z, later, diagonal_rows

            def weights(state, block_scores):
                acc, m0, m1 = state
                ks, z, later, diagonal_rows = block_scores
                vv = split_heads(v_ref[p, ks, :])
                w = mask(jnp.exp2(z - later - lane_mass(m0, m1)), diagonal_rows)
                out = jnp.dot(w.astype(BF16), vv, preferred_element_type=F32)
                t0, t1 = totals(later)
                return acc + out, m0 + t0, m1 + t1

            zero = jnp.zeros((half, 1), F32)
            big = jnp.concatenate([jnp.full((half, 1), 1e30, F32), zero], axis=0)
            lower, upper = scores(first_kb + 1, half), scores(first_kb, 0)
            acc, m0, m1 = weights((jnp.zeros((rows, LANES), F32), big, big), lower)
            m0 = jnp.concatenate([zero, m0[half:]], axis=0)
            m1 = jnp.concatenate([zero, m1[half:]], axis=0)
            state = weights((acc, m0, m1), upper)

            def cond(carry):
                kb, go, _ = carry
                return jnp.logical_and(kb >= ATTN_PAD // keys, go)

            def body(carry):
                kb, _, state = carry
                staged = [scores(kb - n, None) for n in range(SLOW_BLOCKS_PER_STEP)]
                for block_scores in staged:
                    state = weights(state, block_scores)
                return kb - SLOW_BLOCKS_PER_STEP, alive(state[1], state[2]), state

            _, _, state = lax.while_loop(cond, body, (first_kb - 1, True, state))
            o_ref[p, tile_rows, :] = state[0].astype(o_ref.dtype)
        return 0

    @pl.when(flags != 0)
    def _():
        lax.fori_loop(0, pairs * n_sub, slow_pair, 0)


def _attention(qkv):
    batch, n, padded, _ = qkv.shape
    seq = padded - ATTN_PAD
    pairs = n // N_QKV_TILES
    rows, keys = ATTN_STEP_ROWS, ATTN_KEYS
    upper = (jnp.arange(keys)[:, None] >= jnp.arange(keys)[None, :]).astype(BF16)
    zeros = jnp.zeros_like(upper)
    tri = jnp.block([[upper, zeros], [zeros, upper]])
    return pl.pallas_call(
        _attn_kernel,
        out_shape=jax.ShapeDtypeStruct((batch, pairs, seq, LANES), BF16),
        grid=(batch, seq // rows),
        in_specs=[pl.BlockSpec((None, pairs, rows, LANES),
                               lambda b, i: (b, 0, i + ATTN_PAD // ATTN_STEP_ROWS, 0)),
                  pl.BlockSpec((None, pairs, padded, LANES), lambda b, i: (b, 1, 0, 0)),
                  pl.BlockSpec((None, pairs, padded, LANES), lambda b, i: (b, 2, 0, 0)),
                  pl.BlockSpec((2 * keys, 2 * keys), lambda b, i: (0, 0))],
        out_specs=pl.BlockSpec((None, pairs, rows, LANES), lambda b, i: (b, 0, i, 0)),
        compiler_params=pltpu.CompilerParams(dimension_semantics=("arbitrary", "arbitrary"),
                                             vmem_limit_bytes=VMEM_LIMIT),
        name="sb_attention",
    )(qkv, qkv, qkv, tri)


def _silu_bf16(z):
    h = z * jnp.asarray(0.5, BF16)
    return h + h * jnp.tanh(h)


def _sigmoid_bf16(z):
    half = jnp.asarray(0.5, BF16)
    return half + half * jnp.tanh(z * half)


def _tail_kernel(attn_ref, za_ref, d_ref, zb_ref, ma_ref, mb_ref, x_ref, gate_ref,
                 poolw_ref, pscale_ref, wa_ref, wb_ref, wo_ref, fgain_ref,
                 o_ref, pooled_ref):
    pairs = attn_ref.shape[0]
    group = x_ref.shape[1] // len(POOL_WINDOWS)

    attn = jnp.concatenate([attn_ref[p] for p in range(pairs)], axis=1)
    a_in = attn * _silu_bf16(za_ref[...])
    p_a = jnp.dot(a_in, wa_ref[...], preferred_element_type=F32)

    for g in range(len(POOL_WINDOWS)):
        cols = slice(g * group, (g + 1) * group)
        mixed = jnp.dot(d_ref[:, cols], poolw_ref[g], preferred_element_type=F32)
        pooled_ref[:, cols] = (mixed * pscale_ref[:, cols]).astype(BF16)
    b_in = pooled_ref[...] * _silu_bf16(zb_ref[...])
    p_b = jnp.dot(b_in, wb_ref[...], preferred_element_type=F32)

    merged = (_sigmoid_bf16(ma_ref[...]) * p_a.astype(BF16)
              + _sigmoid_bf16(mb_ref[...]) * p_b.astype(BF16))
    out = jnp.dot(merged, wo_ref[...], preferred_element_type=F32)
    y = x_ref[...] + gate_ref[...] * out
    y = y * lax.rsqrt(jnp.mean(y * y, axis=-1, keepdims=True) + EPS)
    o_ref[...] = y * fgain_ref[...]


def _tail(attn, rest, x, gate, pool_w, pool_scale, w_branch_a, w_branch_b, w_out, final_gain):
    batch, seq, d = x.shape
    pairs = attn.shape[1]
    rows = TAIL_ROWS
    n_win = len(POOL_WINDOWS)
    group = d // n_win
    tile = lambda col: pl.BlockSpec((None, rows, d), lambda b, i: (b, i, col))
    full2 = lambda shape: pl.BlockSpec(shape, lambda b, i: (0, 0))
    full3 = lambda shape: pl.BlockSpec(shape, lambda b, i: (0, 0, 0))
    return pl.pallas_call(
        _tail_kernel,
        out_shape=jax.ShapeDtypeStruct((batch, seq, d), F32),
        grid=(batch, seq // rows),
        in_specs=[pl.BlockSpec((None, pairs, rows, LANES), lambda b, i: (b, 0, i, 0)),
                  tile(0),
                  tile(1),
                  tile(2), tile(3), tile(4),
                  tile(0),
                  pl.BlockSpec((None, 1, d), lambda b, i: (b, 0, 0)),
                  full3((n_win, group, group)),
                  full2((1, d)),
                  full2((d, d)), full2((d, d)), full2((d, d)),
                  full2((1, d))],
        out_specs=pl.BlockSpec((None, rows, d), lambda b, i: (b, i, 0)),
        scratch_shapes=[pltpu.VMEM((rows, d), BF16)],
        compiler_params=pltpu.CompilerParams(dimension_semantics=("arbitrary", "arbitrary"),
                                             vmem_limit_bytes=VMEM_LIMIT),
        name="tail",
    )(attn, rest, rest, rest, rest, rest, x, gate.reshape(batch, 1, d),
      pool_w.astype(BF16), pool_scale.reshape(1, d), w_branch_a.astype(BF16),
      w_branch_b.astype(BF16), w_out.astype(BF16), final_gain.reshape(1, d))


def kernel(x, c, norm_gain, w_ada, b_ada, w_in, pool_w, pool_scale, w_branch_a, w_branch_b,
           w_out, final_gain):
    depth = norm_gain.shape[0]
    assert depth == 1, "single-layer block"
    d = x.shape[-1]
    mod = _modulation(c, w_ada[0], b_ada[0])
    shift, scale, gate = mod[:, :d], mod[:, d:2 * d], mod[:, 2 * d:]
    qkv, rest = _projection(x, norm_gain[0], scale, shift, w_in[0].astype(BF16))
    attn = _attention(qkv)
    return _tail(attn, rest, x, gate, pool_w[0], pool_scale[0], w_branch_a[0], w_branch_b[0],
                 w_out[0], final_gain)
```

```python
import jax
import jax.numpy as jnp
from jax import lax
from jax.experimental import pallas as pl
from jax.experimental.pallas import tpu as pltpu

N_HEADS = 16
HEAD_DIM = 64
SB_SCALE = HEAD_DIM ** -0.5
POOL_WINDOWS = (2, 4, 8, 16)
MAX_WINDOW = max(POOL_WINDOWS)
EPS = 1e-6

LANES = 128
PROJ_COL_TILE = 1024
PROJ_ROW_CHUNK = 512
NORM_ROW_CHUNK = 256
ATTN_ROWS = 256
ATTN_STEP_ROWS = 512
ATTN_KEYS = 128
ATTN_PAD = 512
SLOW_BLOCKS_PER_STEP = 4
TAIL_ROWS = 512
VMEM_LIMIT = 48 * 1024 * 1024
PROJ_VMEM_LIMIT = 56 * 1024 * 1024
DEAD_MASS = 104.0
EXP_CLAMP = 88.0

F32 = jnp.float32
BF16 = jnp.bfloat16


def _silu(z):
    return z * jax.nn.sigmoid(z)


def _mod_kernel(c_ref, w_ref, b_ref, o_ref):
    c = c_ref[...]
    o_ref[...] = jnp.dot(_silu(c), w_ref[...], preferred_element_type=F32,
                         precision=lax.Precision.HIGHEST) + b_ref[...]


def _modulation(c, w_ada, b_ada):
    batch, d = c.shape
    n = w_ada.shape[1]
    return pl.pallas_call(
        _mod_kernel,
        out_shape=jax.ShapeDtypeStruct((batch, n), F32),
        grid=(n // d,),
        in_specs=[pl.BlockSpec((batch, d), lambda j: (0, 0)),
                  pl.BlockSpec((d, d), lambda j: (0, j)),
                  pl.BlockSpec((1, d), lambda j: (0, j))],
        out_specs=pl.BlockSpec((batch, d), lambda j: (0, j)),
        compiler_params=pltpu.CompilerParams(dimension_semantics=("arbitrary",),
                                             vmem_limit_bytes=VMEM_LIMIT),
        name="adaln_mod",
    )(c, w_ada, b_ada.reshape(1, n))


N_QKV_TILES = 3
POOL_TILE = 4


def _proj_kernel(x_ref, gain_ref, scale_ref, shift_ref, w_ref, qkv_ref, rest_ref, h_ref):
    seq = x_ref.shape[0]
    j = pl.program_id(1)
    pairs = qkv_ref.shape[0]

    def norm_rows(rows):
        xf = x_ref[rows, :]
        y = xf * lax.rsqrt(jnp.mean(xf * xf, axis=-1, keepdims=True) + EPS)
        return ((y * gain_ref[...]) * (1.0 + scale_ref[...]) + shift_ref[...]).astype(BF16)

    def chunk(m):
        rows = slice(m * PROJ_ROW_CHUNK, (m + 1) * PROJ_ROW_CHUNK)
        return rows, jnp.dot(h_ref[rows, :], w_ref[...].astype(BF16), preferred_element_type=F32)

    def store_qkv(rows, res):
        res = res.astype(BF16)
        for p in range(pairs):
            qkv_ref[p, ATTN_PAD + rows.start:ATTN_PAD + rows.stop, :] = (
                res[:, p * LANES:(p + 1) * LANES])

    zero_pad = jnp.zeros((pairs, ATTN_PAD, LANES), BF16)

    @pl.when(j == 0)
    def _():
        qkv_ref[:, 0:ATTN_PAD, :] = zero_pad
        for m in range(seq // PROJ_ROW_CHUNK):
            rows = slice(m * PROJ_ROW_CHUNK, (m + 1) * PROJ_ROW_CHUNK)
            for r in range(rows.start, rows.stop, NORM_ROW_CHUNK):
                sub = slice(r, r + NORM_ROW_CHUNK)
                h_ref[sub, :] = norm_rows(sub)
            store_qkv(*chunk(m))

    @pl.when(jnp.logical_and(j > 0, j < N_QKV_TILES))
    def _():
        qkv_ref[:, 0:ATTN_PAD, :] = zero_pad
        for m in range(seq // PROJ_ROW_CHUNK):
            store_qkv(*chunk(m))

    @pl.when(jnp.logical_and(j >= N_QKV_TILES, j != POOL_TILE))
    def _():
        for m in range(seq // PROJ_ROW_CHUNK):
            rows, res = chunk(m)
            rest_ref[rows, :] = res.astype(BF16)

    @pl.when(j == POOL_TILE)
    def _():
        group = w_ref.shape[1] // len(POOL_WINDOWS)

        def pool(rows, u, prev):
            pos = lax.broadcasted_iota(jnp.int32, (PROJ_ROW_CHUNK, 1), 0) + (rows.start + 1)
            for g, win in enumerate(POOL_WINDOWS):
                cols = slice(g * group, (g + 1) * group)
                s = jnp.concatenate([prev[:, cols], u[:, cols]], axis=0)
                shift = 1
                while shift < win:
                    s = s + pltpu.roll(s, shift, axis=0)
                    shift *= 2
                inv_count = 1.0 / jnp.minimum(pos, win).astype(F32)
                rest_ref[rows, cols] = (s[MAX_WINDOW:] * inv_count - u[:, cols]).astype(BF16)
            return u[PROJ_ROW_CHUNK - MAX_WINDOW:]

        prev = jnp.zeros((MAX_WINDOW, w_ref.shape[1]), F32)
        for rows, u in [chunk(m) for m in range(seq // PROJ_ROW_CHUNK)]:
            prev = pool(rows, u, prev)


def _projection(x, norm_gain, scale, shift, w_in_bf16):
    batch, seq, d = x.shape
    n = w_in_bf16.shape[1]
    n_tiles = n // PROJ_COL_TILE
    pairs = PROJ_COL_TILE // LANES
    n_rest = n_tiles - N_QKV_TILES
    padded = seq + ATTN_PAD
    return pl.pallas_call(
        _proj_kernel,
        out_shape=(jax.ShapeDtypeStruct((batch, N_QKV_TILES * pairs, padded, LANES), BF16),
                   jax.ShapeDtypeStruct((batch, seq, n_rest * PROJ_COL_TILE), BF16)),
        grid=(batch, n_tiles),
        in_specs=[pl.BlockSpec((None, seq, d), lambda b, j: (b, 0, 0)),
                  pl.BlockSpec((1, d), lambda b, j: (0, 0)),
                  pl.BlockSpec((None, 1, d), lambda b, j: (b, 0, 0)),
                  pl.BlockSpec((None, 1, d), lambda b, j: (b, 0, 0)),
                  pl.BlockSpec((d, PROJ_COL_TILE), lambda b, j: (0, j))],
        out_specs=(pl.BlockSpec((None, pairs, padded, LANES),
                                lambda b, j: (b, jnp.minimum(j, N_QKV_TILES - 1), 0, 0)),
                   pl.BlockSpec((None, seq, PROJ_COL_TILE),
                                lambda b, j: (b, 0, jnp.maximum(j - N_QKV_TILES, 0)))),
        scratch_shapes=[pltpu.VMEM((seq, d), BF16)],
        compiler_params=pltpu.CompilerParams(dimension_semantics=("arbitrary", "arbitrary"),
                                             vmem_limit_bytes=PROJ_VMEM_LIMIT),
        name="in_proj",
    )(x, norm_gain.reshape(1, d), scale.reshape(batch, 1, d), shift.reshape(batch, 1, d),
      w_in_bf16)


def _attn_kernel(q_ref, k_ref, v_ref, tri_ref, o_ref):
    pairs, step_rows, _ = q_ref.shape
    rows = ATTN_ROWS
    n_sub = step_rows // rows
    keys = ATTN_KEYS
    half = rows // 2
    assert half == keys and ATTN_PAD >= 2 * keys and ATTN_PAD % step_rows == 0
    assert SLOW_BLOCKS_PER_STEP * keys <= ATTN_PAD
    tile = 2 * keys
    step = pl.program_id(1)
    first_head = lax.broadcasted_iota(jnp.int32, (1, LANES), 1) < HEAD_DIM
    row = lax.broadcasted_iota(jnp.int32, (half, tile), 0)
    key = lax.broadcasted_iota(jnp.int32, (half, tile), 1) & (keys - 1)
    below_diagonal = key < row
    tri = tri_ref[...]

    def first_block(i):
        return 2 * i + ATTN_PAD // keys

    def split_heads(x):
        zero = jnp.zeros((keys, LANES), x.dtype)
        parts = []
        for n in range(x.shape[0] // keys):
            blk = x[n * keys:(n + 1) * keys]
            parts += [jnp.where(first_head, blk, zero), jnp.where(first_head, zero, blk)]
        return jnp.concatenate(parts, axis=0)

    def softplus(z):
        return jnp.maximum(z, jnp.log(1.0 + jnp.exp(jnp.minimum(z, EXP_CLAMP))))

    def lane_mass(m0, m1):
        n = m0.shape[0]
        return jnp.concatenate([jnp.broadcast_to(m0, (n, keys)),
                                jnp.broadcast_to(m1, (n, keys))], axis=1)

    def totals(later):
        return later[:, 0:1], later[:, keys:keys + 1]

    def alive(m0, m1):
        return jnp.min(jnp.minimum(m0, m1)) < DEAD_MASS

    t = lambda x, n: x[:, n * tile:(n + 1) * tile]

    def fast_scores(p, sub):
        first_kb = first_block(n_sub * step + sub)
        q = q_ref[p, sub * rows:(sub + 1) * rows, :] * SB_SCALE
        win = pl.ds(pl.multiple_of((first_kb - 2) * keys, keys), 4 * keys)
        kk = split_heads(k_ref[p, win, :])
        vv = split_heads(v_ref[p, win, :])
        z = lax.dot_general(q, kk, (((1,), (1,)), ((), ())), preferred_element_type=F32)
        zt, zb = z[:half], z[half:]
        sp_t = [softplus(t(zt, 0)), softplus(t(zt, 1)),
                jnp.where(below_diagonal, softplus(t(zt, 2)), 0.0)]
        sp_b = [softplus(t(zb, 1)), softplus(t(zb, 2)),
                jnp.where(below_diagonal, softplus(t(zb, 3)), 0.0)]
        later_t = jnp.dot(jnp.concatenate(sp_t, axis=0).astype(BF16), tri,
                          preferred_element_type=F32)
        later_b = jnp.dot(jnp.concatenate(sp_b, axis=0).astype(BF16), tri,
                          preferred_element_type=F32)
        l_t = [later_t[n * half:(n + 1) * half] for n in range(3)]
        l_b = [later_b[n * half:(n + 1) * half] for n in range(3)]
        return zt, zb, l_t, l_b, vv

    def fast_weights(p, sub, flags, scores):
        zt, zb, l_t, l_b, vv = scores
        i = n_sub * step + sub
        m2 = totals(l_t[2])
        m1 = tuple(a + b for a, b in zip(m2, totals(l_t[1])))
        m0 = tuple(a + b for a, b in zip(m1, totals(l_t[0])))
        w_t = [jnp.exp(t(zt, 0) - l_t[0] - lane_mass(m1[0], m1[1])),
               jnp.exp(t(zt, 1) - l_t[1] - lane_mass(m2[0], m2[1])),
               jnp.where(below_diagonal, jnp.exp(t(zt, 2) - l_t[2]), 0.0)]
        n3 = totals(l_b[2])
        n2 = tuple(a + b for a, b in zip(n3, totals(l_b[1])))
        n1 = tuple(a + b for a, b in zip(n2, totals(l_b[0])))
        w_b = [jnp.exp(t(zb, 1) - l_b[0] - lane_mass(n2[0], n2[1])),
               jnp.exp(t(zb, 2) - l_b[1] - lane_mass(n3[0], n3[1])),
               jnp.where(below_diagonal, jnp.exp(t(zb, 3) - l_b[2]), 0.0)]
        out_t = jnp.dot(jnp.concatenate(w_t, axis=1).astype(BF16), vv[:3 * tile],
                        preferred_element_type=F32)
        out_b = jnp.dot(jnp.concatenate(w_b, axis=1).astype(BF16), vv[tile:],
                        preferred_element_type=F32)
        o_ref[p, sub * rows:sub * rows + half, :] = out_t.astype(o_ref.dtype)
        o_ref[p, sub * rows + half:(sub + 1) * rows, :] = out_b.astype(o_ref.dtype)
        more = jnp.logical_or(jnp.logical_and(i >= 1, alive(n1[0], n1[1])),
                              jnp.logical_and(i >= 2, alive(m0[0], m0[1])))
        return flags | (more.astype(jnp.int32) << (p + pairs * sub))

    units = [(p, sub) for sub in range(n_sub) for p in range(pairs)]
    staged = [fast_scores(p, sub) for p, sub in units]
    flags = jnp.int32(0)
    for (p, sub), scores in zip(units, staged):
        flags = fast_weights(p, sub, flags, scores)

    def slow_pair(unit, _):
        @pl.when(((flags >> unit) & 1) == 1)
        def _():
            p = unit % pairs
            sub = unit // pairs
            first_kb = first_block(n_sub * step + sub)
            tile_rows = pl.ds(pl.multiple_of(sub * rows, rows), rows)
            q = q_ref[p, tile_rows, :] * SB_SCALE

            def mask(x, diagonal_rows):
                if diagonal_rows is None:
                    return x
                lo = diagonal_rows
                parts = [x[:lo]] if lo else []
                parts.append(jnp.where(below_diagonal, x[lo:lo + half], 0.0))
                if lo + half < rows:
                    parts.append(x[lo + half:])
                return jnp.concatenate(parts, axis=0)

            def scores(kb, diagonal_rows):
                ks = pl.ds(pl.multiple_of(kb * keys, keys), keys)
                kk = split_heads(k_ref[p, ks, :])
                z = lax.dot_general(q, kk, (((1,), (1,)), ((), ())),
                                    preferred_element_type=F32)
                sp = mask(softplus(z), diagonal_rows)
                later = jnp.dot(sp.astype(BF16), tri, preferred_element_type=F32)
                return ks, z, later, diagonal_rows

            def weights(state, block_scores):
                acc, m0, m1 = state
                ks, z, later, diagonal_rows = block_scores
                vv = split_heads(v_ref[p, ks, :])
                w = mask(jnp.exp(z - later - lane_mass(m0, m1)), diagonal_rows)
                out = jnp.dot(w.astype(BF16), vv, preferred_element_type=F32)
                t0, t1 = totals(later)
                return acc + out, m0 + t0, m1 + t1

            zero = jnp.zeros((half, 1), F32)
            big = jnp.concatenate([jnp.full((half, 1), 1e30, F32), zero], axis=0)
            lower, upper = scores(first_kb + 1, half), scores(first_kb, 0)
            acc, m0, m1 = weights((jnp.zeros((rows, LANES), F32), big, big), lower)
            m0 = jnp.concatenate([zero, m0[half:]], axis=0)
            m1 = jnp.concatenate([zero, m1[half:]], axis=0)
            state = weights((acc, m0, m1), upper)

            def cond(carry):
                kb, go, _ = carry
                return jnp.logical_and(kb >= ATTN_PAD // keys, go)

            def body(carry):
                kb, _, state = carry
                staged = [scores(kb - n, None) for n in range(SLOW_BLOCKS_PER_STEP)]
                for block_scores in staged:
                    state = weights(state, block_scores)
                return kb - SLOW_BLOCKS_PER_STEP, alive(state[1], state[2]), state

            _, _, state = lax.while_loop(cond, body, (first_kb - 1, True, state))
            o_ref[p, tile_rows, :] = state[0].astype(o_ref.dtype)
        return 0

    @pl.when(flags != 0)
    def _():
        lax.fori_loop(0, pairs * n_sub, slow_pair, 0)


def _attention(qkv):
    batch, n, padded, _ = qkv.shape
    seq = padded - ATTN_PAD
    pairs = n // N_QKV_TILES
    rows, keys = ATTN_STEP_ROWS, ATTN_KEYS
    upper = (jnp.arange(keys)[:, None] >= jnp.arange(keys)[None, :]).astype(BF16)
    zeros = jnp.zeros_like(upper)
    tri = jnp.block([[upper, zeros], [zeros, upper]])
    return pl.pallas_call(
        _attn_kernel,
        out_shape=jax.ShapeDtypeStruct((batch, pairs, seq, LANES), BF16),
        grid=(batch, seq // rows),
        in_specs=[pl.BlockSpec((None, pairs, rows, LANES),
                               lambda b, i: (b, 0, i + ATTN_PAD // ATTN_STEP_ROWS, 0)),
                  pl.BlockSpec((None, pairs, padded, LANES), lambda b, i: (b, 1, 0, 0)),
                  pl.BlockSpec((None, pairs, padded, LANES), lambda b, i: (b, 2, 0, 0)),
                  pl.BlockSpec((2 * keys, 2 * keys), lambda b, i: (0, 0))],
        out_specs=pl.BlockSpec((None, pairs, rows, LANES), lambda b, i: (b, 0, i, 0)),
        compiler_params=pltpu.CompilerParams(dimension_semantics=("arbitrary", "arbitrary"),
                                             vmem_limit_bytes=VMEM_LIMIT),
        name="sb_attention",
    )(qkv, qkv, qkv, tri)


def _silu_bf16(z):
    h = z * jnp.asarray(0.5, BF16)
    return h + h * jnp.tanh(h)


def _sigmoid_bf16(z):
    half = jnp.asarray(0.5, BF16)
    return half + half * jnp.tanh(z * half)


def _tail_kernel(attn_ref, za_ref, d_ref, zb_ref, ma_ref, mb_ref, x_ref, gate_ref,
                 poolw_ref, pscale_ref, wa_ref, wb_ref, wo_ref, fgain_ref,
                 o_ref, pooled_ref):
    pairs = attn_ref.shape[0]
    group = x_ref.shape[1] // len(POOL_WINDOWS)

    attn = jnp.concatenate([attn_ref[p] for p in range(pairs)], axis=1)
    a_in = attn * _silu_bf16(za_ref[...])
    p_a = jnp.dot(a_in, wa_ref[...], preferred_element_type=F32)

    for g in range(len(POOL_WINDOWS)):
        cols = slice(g * group, (g + 1) * group)
        mixed = jnp.dot(d_ref[:, cols], poolw_ref[g], preferred_element_type=F32)
        pooled_ref[:, cols] = (mixed * pscale_ref[:, cols]).astype(BF16)
    b_in = pooled_ref[...] * _silu_bf16(zb_ref[...])
    p_b = jnp.dot(b_in, wb_ref[...], preferred_element_type=F32)

    merged = (_sigmoid_bf16(ma_ref[...]) * p_a.astype(BF16)
              + _sigmoid_bf16(mb_ref[...]) * p_b.astype(BF16))
    out = jnp.dot(merged, wo_ref[...], preferred_element_type=F32)
    y = x_ref[...] + gate_ref[...] * out
    y = y * lax.rsqrt(jnp.mean(y * y, axis=-1, keepdims=True) + EPS)
    o_ref[...] = y * fgain_ref[...]


def _tail(attn, rest, x, gate, pool_w, pool_scale, w_branch_a, w_branch_b, w_out, final_gain):
    batch, seq, d = x.shape
    pairs = attn.shape[1]
    rows = TAIL_ROWS
    n_win = len(POOL_WINDOWS)
    group = d // n_win
    tile = lambda col: pl.BlockSpec((None, rows, d), lambda b, i: (b, i, col))
    full2 = lambda shape: pl.BlockSpec(shape, lambda b, i: (0, 0))
    full3 = lambda shape: pl.BlockSpec(shape, lambda b, i: (0, 0, 0))
    return pl.pallas_call(
        _tail_kernel,
        out_shape=jax.ShapeDtypeStruct((batch, seq, d), F32),
        grid=(batch, seq // rows),
        in_specs=[pl.BlockSpec((None, pairs, rows, LANES), lambda b, i: (b, 0, i, 0)),
                  tile(0),
                  tile(1),
                  tile(2), tile(3), tile(4),
                  tile(0),
                  pl.BlockSpec((None, 1, d), lambda b, i: (b, 0, 0)),
                  full3((n_win, group, group)),
                  full2((1, d)),
                  full2((d, d)), full2((d, d)), full2((d, d)),
                  full2((1, d))],
        out_specs=pl.BlockSpec((None, rows, d), lambda b, i: (b, i, 0)),
        scratch_shapes=[pltpu.VMEM((rows, d), BF16)],
        compiler_params=pltpu.CompilerParams(dimension_semantics=("arbitrary", "arbitrary"),
                                             vmem_limit_bytes=VMEM_LIMIT),
        name="tail",
    )(attn, rest, rest, rest, rest, rest, x, gate.reshape(batch, 1, d),
      pool_w.astype(BF16), pool_scale.reshape(1, d), w_branch_a.astype(BF16),
      w_branch_b.astype(BF16), w_out.astype(BF16), final_gain.reshape(1, d))


def kernel(x, c, norm_gain, w_ada, b_ada, w_in, pool_w, pool_scale, w_branch_a, w_branch_b,
           w_out, final_gain):
    depth = norm_gain.shape[0]
    assert depth == 1, "single-layer block"
    d = x.shape[-1]
    mod = _modulation(c, w_ada[0], b_ada[0])
    shift, scale, gate = mod[:, :d], mod[:, d:2 * d], mod[:, 2 * d:]
    qkv, rest = _projection(x, norm_gain[0], scale, shift, w_in[0])
    attn = _attention(qkv)
    return _tail(attn, rest, x, gate, pool_w[0], pool_scale[0], w_branch_a[0], w_branch_b[0],
                 w_out[0], final_gain)
```
